```python
import jax, jax.numpy as jnp
from jax import lax
import numpy as np

D_MODEL = 1024
BATCH = 16
SEQ = 2048
DEPTH = 2

D_MIX = D_MODEL
N_MIXERS = 4
GROUP_W = D_MIX // N_MIXERS
HEAD_DIM = 64
N_HEADS_GROUP = GROUP_W // HEAD_DIM
QK_SCALE = HEAD_DIM ** -0.5
POOL_WINDOWS = (2, 4, 8, 16)
POOL_CH = GROUP_W // len(POOL_WINDOWS)
MAX_WINDOW = 16
RWKV_DECAY_LORA = 64
RWKV_A_LORA = 64
RWKV_GATE_LORA = 128
RWKV_COLS = 3 * GROUP_W + RWKV_DECAY_LORA + RWKV_A_LORA + RWKV_GATE_LORA
RWKV_GN_EPS = 64e-5
GLA_GATE_LORA = 16
GLA_GATE_TAU = 16.0
GLA_COLS = 4 * GROUP_W + GLA_GATE_LORA
CHUNK = 64
D_IN = GROUP_W + 4 * GROUP_W + RWKV_COLS + GLA_COLS
D_FF = 2816
NORM_EPS = 1e-6
GATE_FLOOR = 1e-30

kernel_name = 'hymba_style_pool_hgrn2_rwkv7_gla_macaron'


def rms_norm(x, g, eps=NORM_EPS):
    xf = x.astype(jnp.float32)
    y = xf * lax.rsqrt(jnp.mean(xf * xf, axis=-1, keepdims=True) + eps)
    return (y * g.astype(jnp.float32)).astype(x.dtype)


def swiglu_ffn(x, w_in, w_out):
    gate, up = jnp.split(x @ w_in, 2, axis=-1)
    return (jax.nn.silu(gate) * up) @ w_out


def split_heads(t):
    return t.reshape(t.shape[:-1] + (N_HEADS_GROUP, HEAD_DIM))


def causal_multiscale_pool(p, w_pool, b_pool, scale):
    B, T, _ = p.shape
    cs = jnp.pad(jnp.cumsum(p, axis=1), ((0, 0), (MAX_WINDOW, 0), (0, 0)))
    pos = jnp.arange(T)
    pooled = []
    for gi, w in enumerate(POOL_WINDOWS):
        lo, hi = gi * POOL_CH, (gi + 1) * POOL_CH
        win_sum = cs[:, MAX_WINDOW:, lo:hi] - cs[:, MAX_WINDOW - w:MAX_WINDOW - w + T, lo:hi]
        count = jnp.minimum(pos + 1, w).astype(jnp.float32)[None, :, None]
        pooled.append(win_sum / count)
    pooled = jnp.stack(pooled, axis=2) - p.reshape(B, T, len(POOL_WINDOWS), POOL_CH)
    y = jnp.einsum('btgc,gcd->btgd', pooled, w_pool).reshape(B, T, GROUP_W) + b_pool
    return y * scale


def chunked_gated_linear_attention(q, k, v, log_a):
    B, T, H, K = q.shape
    V = v.shape[-1]
    n = T // CHUNK

    def to_chunks(t):
        return t.reshape(B, n, CHUNK, H, t.shape[-1]).transpose(1, 0, 3, 2, 4)

    causal = jnp.tril(jnp.ones((CHUNK, CHUNK), dtype=bool))[:, :, None]

    def step(S, inp):
        qc, kc, vc, gc = inp
        b = jnp.cumsum(gc, axis=-2)
        diff = b[:, :, :, None, :] - b[:, :, None, :, :]
        decay = jnp.where(causal, jnp.exp(jnp.where(causal, diff, 0.0)), 0.0)
        scores = jnp.einsum('bhtk,bhsk,bhtsk->bhts', qc, kc, decay)
        o = jnp.einsum('bhts,bhsv->bhtv', scores, vc) + jnp.einsum('bhtk,bhkv->bhtv', qc * jnp.exp(b), S)
        b_last = b[:, :, -1:, :]
        S_new = jnp.exp(b_last[:, :, 0, :])[..., None] * S + jnp.einsum('bhsk,bhsv->bhkv', kc * jnp.exp(b_last - b), vc)
        return S_new, o

    S0 = jnp.zeros((B, H, K, V), jnp.float32)
    _, o = lax.scan(step, S0, (to_chunks(q), to_chunks(k), to_chunks(v), to_chunks(log_a)))
    return o.transpose(1, 0, 3, 2, 4).reshape(B, T, H, V)


def rwkv7_scan(r, w, k, v, kk, a):
    B, T, H, N = r.shape

    def step(S, inp):
        r_t, w_t, k_t, v_t, kk_t, a_t = inp
        s_kk = jnp.einsum('bhvk,bhk->bhv', S, kk_t)
        S = (S * w_t[:, :, None, :] - s_kk[..., None] * (kk_t * a_t)[:, :, None, :]
             + v_t[..., :, None] * k_t[:, :, None, :])
        return S, jnp.einsum('bhvk,bhk->bhv', S, r_t)

    xs = tuple(jnp.moveaxis(t, 1, 0) for t in (r, w, k, v, kk, a))
    S0 = jnp.zeros((B, H, N, N), jnp.float32)
    _, out = lax.scan(step, S0, xs)
    return jnp.moveaxis(out, 0, 1)


def token_mixer(u, w_in, w_out, pool_w, pool_b, pool_scale, lb, hgrn_norm,
                rwkv_mu, rwkv_w0, rwkv_w2, rwkv_a0, rwkv_a2, rwkv_g2, rwkv_k_k, rwkv_k_a,
                rwkv_r_k, rwkv_ln_w, rwkv_ln_b, gla_w2, gla_b, gla_norm):
    B, T, _ = u.shape
    G = GROUP_W
    p = (u @ w_in).astype(jnp.float32)
    p_pool, p_hg, p_rw, p_gla = jnp.split(p, [G, 5 * G, 5 * G + RWKV_COLS], axis=-1)

    y_pool = causal_multiscale_pool(p_pool, pool_w, pool_b, pool_scale)

    hq, hf, hi, hg = jnp.split(p_hg, 4, axis=-1)
    f = lb + (1.0 - lb) * jax.nn.sigmoid(hf)
    log_f = jnp.log(jnp.maximum(f, GATE_FLOOR))
    o_h = chunked_gated_linear_attention(split_heads(jax.nn.silu(hq) * QK_SCALE),
                                         split_heads(1.0 - f),
                                         split_heads(hi), split_heads(log_f)).reshape(B, T, G)
    y_hgrn = rms_norm(o_h, hgrn_norm) * jax.nn.sigmoid(hg)

    prev = jnp.pad(p_rw, ((0, 0), (1, 0), (0, 0)))[:, :T]
    p_rw = p_rw + (prev - p_rw) * rwkv_mu
    rr, rk, rv, xw, xa, xg = jnp.split(
        p_rw, [G, 2 * G, 3 * G, 3 * G + RWKV_DECAY_LORA, 3 * G + RWKV_DECAY_LORA + RWKV_A_LORA], axis=-1)
    w_log = -jax.nn.softplus(-(rwkv_w0 + jnp.tanh(xw) @ rwkv_w2)) - 0.5
    decay = jnp.exp(-jnp.exp(w_log))
    a = jax.nn.sigmoid(rwkv_a0 + xa @ rwkv_a2)
    g_r = jax.nn.sigmoid(xg) @ rwkv_g2
    kk = split_heads(rk * rwkv_k_k)
    kk = kk / jnp.maximum(jnp.sqrt(jnp.sum(kk * kk, axis=-1, keepdims=True)), 1e-12)
    rk = rk * (1.0 + (a - 1.0) * rwkv_k_a)
    r_h, k_h, v_h = split_heads(rr), split_heads(rk), split_heads(rv)
    o_r = rwkv7_scan(r_h, split_heads(decay), k_h, v_h, kk, split_heads(a))
    mu = jnp.mean(o_r, axis=-1, keepdims=True)
    var = jnp.mean(jnp.square(o_r - mu), axis=-1, keepdims=True)
    gn = ((o_r - mu) * lax.rsqrt(var + RWKV_GN_EPS)).reshape(B, T, G) * rwkv_ln_w + rwkv_ln_b
    bonus = (jnp.sum(r_h * k_h * split_heads(rwkv_r_k), axis=-1, keepdims=True) * v_h).reshape(B, T, G)
    y_rwkv = (gn + bonus) * g_r

    gq, gk, gv, gg, ga = jnp.split(p_gla, [G, 2 * G, 3 * G, 4 * G], axis=-1)
    log_a = jax.nn.log_sigmoid(ga @ gla_w2 + gla_b) / GLA_GATE_TAU
    o_g = chunked_gated_linear_attention(split_heads(gq * QK_SCALE), split_heads(gk),
                                         split_heads(gv), split_heads(log_a))
    o_g = (o_g * lax.rsqrt(jnp.mean(o_g * o_g, axis=-1, keepdims=True) + NORM_EPS)).reshape(B, T, G)
    y_gla = o_g * gla_norm * jax.nn.silu(gg)

    y = jnp.concatenate([y_pool, y_hgrn, y_rwkv, y_gla], axis=-1).astype(u.dtype)
    return y @ w_out


def setup_inputs(seed: int = 0) -> dict:
    key = jax.random.key(seed)
    ks = jax.random.split(key, 32)
    f32 = jnp.float32
    L, D, G = DEPTH, D_MODEL, GROUP_W

    def dense(k, shape, fan_in):
        return jax.random.normal(k, shape, f32) * fan_in ** -0.5

    def gain(k, shape):
        return 1.0 + 0.1 * jax.random.normal(k, shape, f32)

    def small(k, shape, s):
        return s * jax.random.normal(k, shape, f32)

    return {
        'x': jax.random.normal(ks[0], (BATCH, SEQ, D), f32),
        'norm_ffn1': gain(ks[1], (L, D)),
        'ffn1_w_in': dense(ks[2], (L, D, 2 * D_FF), D),
        'ffn1_w_out': dense(ks[3], (L, D_FF, D), D_FF),
        'norm_mix': gain(ks[4], (L, D)),
        'w_in': dense(ks[5], (L, D, D_IN), D),
        'w_out': dense(ks[6], (L, D_MIX, D), D_MIX),
        'pool_w': dense(ks[7], (L, len(POOL_WINDOWS), POOL_CH, POOL_CH), POOL_CH),
        'pool_b': small(ks[8], (L, G), 0.01),
        'pool_scale': gain(ks[9], (L, G)),
        'hgrn_lb_logits': small(ks[10], (L, G), 0.5),
        'hgrn_norm': gain(ks[11], (L, G)),
        'rwkv_mu': jax.random.uniform(ks[12], (L, RWKV_COLS), f32),
        'rwkv_w0': jax.random.uniform(ks[13], (L, G), f32, -4.0, 0.0),
        'rwkv_w2': dense(ks[14], (L, RWKV_DECAY_LORA, G), RWKV_DECAY_LORA),
        'rwkv_a0': small(ks[15], (L, G), 0.5),
        'rwkv_a2': dense(ks[16], (L, RWKV_A_LORA, G), RWKV_A_LORA),
        'rwkv_g2': dense(ks[17], (L, RWKV_GATE_LORA, G), RWKV_GATE_LORA),
        'rwkv_k_k': 0.85 + small(ks[18], (L, G), 0.1),
        'rwkv_k_a': gain(ks[19], (L, G)),
        'rwkv_r_k': small(ks[20], (L, G), 0.1),
        'rwkv_ln_w': gain(ks[21], (L, G)),
        'rwkv_ln_b': small(ks[22], (L, G), 0.01),
        'gla_w2': dense(ks[23], (L, GLA_GATE_LORA, G), GLA_GATE_LORA),
        'gla_b': small(ks[24], (L, G), 0.5),
        'gla_norm': gain(ks[25], (L, G)),
        'norm_ffn2': gain(ks[26], (L, D)),
        'ffn2_w_in': dense(ks[27], (L, D, 2 * D_FF), D),
        'ffn2_w_out': dense(ks[28], (L, D_FF, D), D_FF),
        'norm_final': gain(ks[29], (D,)),
    }


def reference(x, norm_ffn1, ffn1_w_in, ffn1_w_out, norm_mix, w_in, w_out, pool_w, pool_b,
              pool_scale, hgrn_lb_logits, hgrn_norm, rwkv_mu, rwkv_w0, rwkv_w2, rwkv_a0, rwkv_a2,
              rwkv_g2, rwkv_k_k, rwkv_k_a, rwkv_r_k, rwkv_ln_w, rwkv_ln_b, gla_w2, gla_b, gla_norm,
              norm_ffn2, ffn2_w_in, ffn2_w_out, norm_final):
    sm = jax.nn.softmax(hgrn_lb_logits.astype(jnp.float32), axis=0)
    lower_bounds = jnp.cumsum(sm, axis=0) - sm[0]
    for l in range(DEPTH):
        x = x + 0.5 * swiglu_ffn(rms_norm(x, norm_ffn1[l]), ffn1_w_in[l], ffn1_w_out[l])
        x = x + token_mixer(rms_norm(x, norm_mix[l]), w_in[l], w_out[l], pool_w[l], pool_b[l],
                            pool_scale[l], lower_bounds[l], hgrn_norm[l], rwkv_mu[l], rwkv_w0[l],
                            rwkv_w2[l], rwkv_a0[l], rwkv_a2[l], rwkv_g2[l], rwkv_k_k[l], rwkv_k_a[l],
                            rwkv_r_k[l], rwkv_ln_w[l], rwkv_ln_b[l], gla_w2[l], gla_b[l], gla_norm[l])
        x = x + 0.5 * swiglu_ffn(rms_norm(x, norm_ffn2[l]), ffn2_w_in[l], ffn2_w_out[l])
    return rms_norm(x, norm_final)
```

```python
import functools

import numpy as np
import jax
import jax.numpy as jnp
from jax import lax
from jax.experimental import pallas as pl
from jax.experimental.pallas import tpu as pltpu

D_MODEL = 1024
GROUP_W = 256
HEAD_DIM = 64
N_HEADS = GROUP_W // HEAD_DIM
QK_SCALE = HEAD_DIM ** -0.5
POOL_WINDOWS = (2, 4, 8, 16)
POOL_CH = GROUP_W // len(POOL_WINDOWS)
MAX_WINDOW = 16
RWKV_DECAY_LORA = 64
RWKV_A_LORA = 64
RWKV_GATE_LORA = 128
RWKV_GN_EPS = 64e-5
GLA_GATE_LORA = 16
GLA_GATE_TAU = 16.0
D_IN = 13 * GROUP_W + GLA_GATE_LORA
D_FF = 2816
NORM_EPS = 1e-6
GATE_FLOOR = 1e-30

LANES = 128
VMEM_LIMIT_BYTES = 56 * 1024 * 1024

D_IN_PAD = 13 * GROUP_W + LANES
CHUNK = 64
LEVELS = CHUNK.bit_length() - 1
TIME_TILE = 256
ROW_TILE = 512
FF_BLOCK = D_FF // 2

_P_POOL = 0
_P_HG = GROUP_W
_P_RW = 5 * GROUP_W
_P_GLA = 9 * GROUP_W

_F_HQ, _F_HK, _F_HV, _F_HGD = 0, 1, 2, 3
_F_RR, _F_RK, _F_RV, _F_RA, _F_RB, _F_RW = 4, 5, 6, 7, 8, 9
_F_GQ, _F_GK, _F_GV, _F_GGD = 10, 11, 12, 13
_N_FEAT = 14

(_V_POOL_B, _V_POOL_SCALE, _V_HGRN_NORM, _V_W0, _V_A0, _V_KK, _V_KA, _V_RK, _V_LNW, _V_LNB,
 _V_GLA_B, _V_GLA_NORM) = range(12)
_N_VEC = 16

_MXU_DTYPE = jnp.bfloat16
_F32 = jnp.float32


def _mx(a):
    return a.astype(_MXU_DTYPE)


def _dot(a, b):
    return jnp.dot(_mx(a), _mx(b), preferred_element_type=_F32)


def _dot_nt(a, b):
    return lax.dot_general(_mx(a), _mx(b), (((1,), (1,)), ((), ())), preferred_element_type=_F32)


def _dot_tn(a, b):
    return lax.dot_general(_mx(a), _mx(b), (((0,), (0,)), ((), ())), preferred_element_type=_F32)


def _split(x, parts):
    out = []
    r = x
    for _ in range(parts - 1):
        h = r.astype(_MXU_DTYPE)
        out.append(h)
        r = r - h.astype(_F32)
    out.append(r.astype(_MXU_DTYPE))
    return out


def _dot_exact_lhs(m, x, parts=3):
    m = _mx(m)
    acc = None
    for piece in _split(x, parts):
        t = jnp.dot(m, piece, preferred_element_type=_F32)
        acc = t if acc is None else acc + t
    return acc


def _dot_exact_rhs(x, m, parts=2):
    m = _mx(m)
    acc = None
    for piece in _split(x, parts):
        t = jnp.dot(piece, m, preferred_element_type=_F32)
        acc = t if acc is None else acc + t
    return acc


def _rms_norm(x, g, eps=NORM_EPS):
    return x * lax.rsqrt(jnp.mean(x * x, axis=-1, keepdims=True) + eps) * g


def _sigmoid(x):
    return 1.0 / (1.0 + jnp.exp(-x))


def _softplus(x):
    return jnp.maximum(x, 0.0) + jnp.log(1.0 + jnp.exp(-jnp.abs(x)))


def _head_masks():
    lane = lax.broadcasted_iota(jnp.int32, (1, GROUP_W), 1)
    return [(lax.shift_right_logical(lane, 6) == h).astype(_F32) for h in range(N_HEADS)]


def _stack_heads(x, hms):
    return jnp.concatenate([x * hm for hm in hms], axis=0)


def _sum_heads(x, rows):
    out = x[0:rows]
    for h in range(1, N_HEADS):
        out = out + x[h * rows:(h + 1) * rows]
    return out


def _gla_chunk(q, k, v, g, st, sel, hms, xr, bd):
    c = CHUNK
    e = jnp.exp(_dot_exact_lhs(sel, g))
    f_b = e[LEVELS * c:(LEVELS + 1) * c]
    f_a = e[(LEVELS + 1) * c:(LEVELS + 2) * c]
    o = _dot_nt(q * f_b, st)
    st_new = st * f_b[c - 1:c, :] + jnp.where(bd, _dot_tn(v, k * f_a), 0.0)

    t_idx = lax.broadcasted_iota(jnp.int32, (c, GROUP_W), 0)
    sc = jnp.where(xr < 1, _dot_nt(_stack_heads(q, hms), k), 0.0)
    for j in range(1, LEVELS + 1):
        f_j = e[(j - 1) * c:j * c]
        upper = (t_idx & ((1 << j) - 1)) >= (1 << (j - 1))
        qt = jnp.where(upper, q * f_j, 0.0)
        kt = jnp.where(upper, 0.0, k * f_j)
        s_j = _dot_nt(_stack_heads(qt, hms), kt)
        sc = sc + (s_j if j == LEVELS else jnp.where(xr < (1 << j), s_j, 0.0))
    for h in range(N_HEADS):
        o = o + _dot(sc[h * c:(h + 1) * c], v * hms[h])
    return o, st_new


def _rwkv_chunk(r, k, v, al, be, lw, pt, tri, hms, row_b, col_b, bd):
    c = CHUNK
    cw = _dot_exact_lhs(tri, lw)
    cw_last = cw[c - 1:c, :]
    e_neg = jnp.exp(-cw)
    e_rem = jnp.exp(cw_last - cw)
    a_t = al * jnp.exp(cw - lw)
    b_t = be * e_neg
    k_t = k * e_neg
    r_t = r * jnp.exp(cw)

    a_st = _stack_heads(a_t, hms)
    r_st = _stack_heads(r_t, hms)
    b_tl = jnp.concatenate([b_t] * N_HEADS, axis=0)
    k_tl = jnp.concatenate([k_t] * N_HEADS, axis=0)
    same = (row_b ^ col_b) < c
    strict = same & (col_b < row_b)
    incl = same & (col_b <= row_b)
    l_m = jnp.where(strict, _dot_nt(a_st, b_tl), 0.0)
    m_ak = jnp.where(strict, _dot_nt(a_st, k_tl), 0.0)
    m_rk = jnp.where(incl, _dot_nt(r_st, k_tl), 0.0)
    m_rb = jnp.where(incl, _dot_nt(r_st, b_tl), 0.0)
    v_bd = _stack_heads(v, hms)

    x = _dot_nt(a_st, pt) + _dot(m_ak, v_bd)
    x = x - _dot(l_m, x)
    l_p = l_m
    for _ in range(LEVELS - 1):
        l_p = _dot(l_p, l_p)
        x = x + _dot(l_p, x)
    o_bd = _dot_nt(r_st, pt) + _dot(m_rk, v_bd) - _dot(m_rb, x)
    o = _sum_heads(o_bd, c)
    u = _sum_heads(x, c)
    upd = _dot_tn(v, k * e_rem) - _dot_tn(u, be * e_rem)
    pt_new = pt * jnp.exp(cw_last) + jnp.where(bd, upd, 0.0)
    return o, pt_new


def _mixer_kernel(p_ref, sel_ref, tri_ref, seg_ref, poolw_ref, w2_ref, a2_ref, g2_ref, glaw2_ref,
                  mu_ref, vec_ref, lb_ref, y_ref,
                  feat, osc, pool_prev, rw_prev, hg_st, rw_pt, gl_st, *, layer):
    tb = TIME_TILE
    gw = GROUP_W
    ti = pl.program_id(1)

    @pl.when(ti == 0)
    def _():
        pool_prev[...] = jnp.zeros_like(pool_prev)
        rw_prev[...] = jnp.zeros_like(rw_prev)
        hg_st[...] = jnp.zeros_like(hg_st)
        rw_pt[...] = jnp.zeros_like(rw_pt)
        gl_st[...] = jnp.zeros_like(gl_st)

    def vec(i):
        return vec_ref[i:i + 1, :]

    def pcol(base, i, n=1):
        return p_ref[0, :, base + i * gw: base + (i + n) * gw]

    seg = seg_ref[...]
    hms = _head_masks()

    p_pool = pcol(_P_POOL, 0)
    ext = jnp.concatenate([pool_prev[...], p_pool], axis=0)
    pool_prev[...] = p_pool[tb - MAX_WINDOW:, :]
    lane = lax.broadcasted_iota(jnp.int32, (tb, gw), 1)
    grp = lax.shift_right_logical(lane, 6)
    tpos = lax.broadcasted_iota(jnp.int32, (tb, gw), 0) + ti * tb
    win = ext
    win_sum = jnp.zeros((tb, gw), _F32)
    count = jnp.zeros((tb, gw), _F32)
    for gi, w in enumerate(POOL_WINDOWS):
        sh = w // 2
        while sh < w:
            win = win + pltpu.roll(win, sh, 0)
            sh *= 2
        win_sum = jnp.where(grp == gi, win[MAX_WINDOW:, :], win_sum)
        count = jnp.where(grp == gi, jnp.minimum(tpos + 1, w).astype(_F32), count)
    pooled = win_sum / count - p_pool
    y_pool = (_dot(pooled, poolw_ref[...]) + vec(_V_POOL_B)) * vec(_V_POOL_SCALE)
    y_ref[0, :, 0:gw] = y_pool.astype(y_ref.dtype)

    sm = jnp.exp(lb_ref[...] - jnp.max(lb_ref[...], axis=0, keepdims=True))
    sm = sm / jnp.sum(sm, axis=0, keepdims=True)
    lb = jnp.sum(sm[0:layer + 1], axis=0, keepdims=True) - sm[0:1]
    hq = pcol(_P_HG, 0)
    f = lb + (1.0 - lb) * _sigmoid(pcol(_P_HG, 1))
    feat[:, _F_HQ * gw:(_F_HQ + 1) * gw] = hq * _sigmoid(hq) * QK_SCALE
    feat[:, _F_HK * gw:(_F_HK + 1) * gw] = 1.0 - f
    feat[:, _F_HV * gw:(_F_HV + 1) * gw] = pcol(_P_HG, 2)
    feat[:, _F_HGD * gw:(_F_HGD + 1) * gw] = jnp.log(jnp.maximum(f, GATE_FLOOR))

    p_rw = pcol(_P_RW, 0, 4)
    row0 = lax.broadcasted_iota(jnp.int32, p_rw.shape, 0) == 0
    prev = jnp.where(row0, rw_prev[...], pltpu.roll(p_rw, 1, 0))
    rw_prev[...] = p_rw[tb - 1:tb, :]
    p_rw = p_rw + (prev - p_rw) * mu_ref[...]
    rr = p_rw[:, 0:gw]
    rk = p_rw[:, gw:2 * gw]
    rv = p_rw[:, 2 * gw:3 * gw]
    xwa = p_rw[:, 3 * gw:3 * gw + LANES]
    xg = p_rw[:, 3 * gw + LANES:4 * gw]
    w_log = -_softplus(-(vec(_V_W0) + _dot(jnp.tanh(xwa), w2_ref[...]))) - 0.5
    a = _sigmoid(vec(_V_A0) + _dot(xwa, a2_ref[...]))
    g_r = _dot(_sigmoid(xg), g2_ref[...])
    kk = rk * vec(_V_KK)
    kk = kk / jnp.maximum(jnp.sqrt(_dot_exact_rhs(kk * kk, seg) * HEAD_DIM), 1e-12)
    rk = rk * (1.0 + (a - 1.0) * vec(_V_KA))
    feat[:, _F_RR * gw:(_F_RR + 1) * gw] = rr
    feat[:, _F_RK * gw:(_F_RK + 1) * gw] = rk
    feat[:, _F_RV * gw:(_F_RV + 1) * gw] = rv
    feat[:, _F_RA * gw:(_F_RA + 1) * gw] = kk
    feat[:, _F_RB * gw:(_F_RB + 1) * gw] = kk * a
    feat[:, _F_RW * gw:(_F_RW + 1) * gw] = -jnp.exp(w_log)
    bonus = _dot_exact_rhs(rr * rk * vec(_V_RK), seg) * HEAD_DIM * rv

    ga = p_ref[0, :, _P_GLA + 4 * gw:_P_GLA + 4 * gw + LANES]
    z = _dot(ga, glaw2_ref[...]) + vec(_V_GLA_B)
    feat[:, _F_GQ * gw:(_F_GQ + 1) * gw] = pcol(_P_GLA, 0) * QK_SCALE
    feat[:, _F_GK * gw:(_F_GK + 1) * gw] = pcol(_P_GLA, 1)
    feat[:, _F_GV * gw:(_F_GV + 1) * gw] = pcol(_P_GLA, 2)
    feat[:, _F_GGD * gw:(_F_GGD + 1) * gw] = -_softplus(-z) / GLA_GATE_TAU

    c = CHUNK
    hc = N_HEADS * c
    xr = (lax.broadcasted_iota(jnp.int32, (hc, c), 0) & (c - 1)) ^ lax.broadcasted_iota(jnp.int32, (hc, c), 1)
    row_b = lax.broadcasted_iota(jnp.int32, (hc, hc), 0)
    col_b = lax.broadcasted_iota(jnp.int32, (hc, hc), 1)
    bd = (lax.shift_right_logical(lax.broadcasted_iota(jnp.int32, (gw, gw), 0), 6)
          == lax.shift_right_logical(lax.broadcasted_iota(jnp.int32, (gw, gw), 1), 6))
    sel = sel_ref[...]
    tri = tri_ref[...]

    def chunk_body(ci, carry):
        rows = pl.ds(pl.multiple_of(ci * c, c), c)

        def ft(i):
            return feat[rows, i * gw:(i + 1) * gw]

        o_h, st = _gla_chunk(ft(_F_HQ), ft(_F_HK), ft(_F_HV), ft(_F_HGD), hg_st[...], sel, hms, xr, bd)
        hg_st[...] = st
        osc[rows, 0:gw] = o_h
        o_r, pt = _rwkv_chunk(ft(_F_RR), ft(_F_RK), ft(_F_RV), ft(_F_RA), ft(_F_RB), ft(_F_RW),
                              rw_pt[...], tri, hms, row_b, col_b, bd)
        rw_pt[...] = pt
        osc[rows, gw:2 * gw] = o_r
        o_g, st = _gla_chunk(ft(_F_GQ), ft(_F_GK), ft(_F_GV), ft(_F_GGD), gl_st[...], sel, hms, xr, bd)
        gl_st[...] = st
        osc[rows, 2 * gw:3 * gw] = o_g
        return carry

    lax.fori_loop(0, tb // c, chunk_body, 0)

    o_h = osc[:, 0:gw]
    y_hg = _rms_norm(o_h, vec(_V_HGRN_NORM)) * _sigmoid(pcol(_P_HG, 3))
    y_ref[0, :, gw:2 * gw] = y_hg.astype(y_ref.dtype)

    o_r = osc[:, gw:2 * gw]
    mean = _dot_exact_rhs(o_r, seg)
    cen = o_r - mean
    var = _dot_exact_rhs(cen * cen, seg)
    gn = cen * lax.rsqrt(var + RWKV_GN_EPS) * vec(_V_LNW) + vec(_V_LNB)
    y_ref[0, :, 2 * gw:3 * gw] = ((gn + bonus) * g_r).astype(y_ref.dtype)

    o_g = osc[:, 2 * gw:3 * gw]
    ms = _dot_exact_rhs(o_g * o_g, seg)
    gg = pcol(_P_GLA, 3)
    y_gl = o_g * lax.rsqrt(ms + NORM_EPS) * vec(_V_GLA_NORM) * (gg * _sigmoid(gg))
    y_ref[0, :, 3 * gw:4 * gw] = y_gl.astype(y_ref.dtype)


def _selection_matrix():
    c = CHUNK
    m = np.zeros(((LEVELS + 2) * c, c), np.float32)
    for j in range(1, LEVELS + 1):
        n = 1 << j
        for t in range(c):
            mid = (t // n) * n + n // 2 - 1
            if t % n >= n // 2:
                m[(j - 1) * c + t, mid + 1:t + 1] = 1.0
            else:
                m[(j - 1) * c + t, t + 1:mid + 1] = 1.0
    for t in range(c):
        m[LEVELS * c + t, :t + 1] = 1.0
        m[(LEVELS + 1) * c + t, t + 1:] = 1.0
    return m


def _const_spec(shape):
    nd = len(shape)
    return pl.BlockSpec(shape, lambda *_: (0,) * nd, pipeline_mode=pl.Buffered(1))


def _mixer_call(p, consts, layer):
    b, t, _ = p.shape
    tb = TIME_TILE
    gw = GROUP_W
    in_specs = [pl.BlockSpec((1, tb, D_IN_PAD), lambda bi, ti: (bi, ti, 0))]
    in_specs += [_const_spec(a.shape) for a in consts]
    return pl.pallas_call(
        functools.partial(_mixer_kernel, layer=layer),
        grid=(b, t // tb),
        in_specs=in_specs,
        out_specs=pl.BlockSpec((1, tb, D_MODEL), lambda bi, ti: (bi, ti, 0)),
        out_shape=jax.ShapeDtypeStruct((b, t, D_MODEL), _MXU_DTYPE),
        scratch_shapes=[
            pltpu.VMEM((tb, _N_FEAT * gw), _F32),
            pltpu.VMEM((tb, 3 * gw), _F32),
            pltpu.VMEM((MAX_WINDOW, gw), _F32),
            pltpu.VMEM((1, 4 * gw), _F32),
            pltpu.VMEM((gw, gw), _F32),
            pltpu.VMEM((gw, gw), _F32),
            pltpu.VMEM((gw, gw), _F32),
        ],
        compiler_params=pltpu.CompilerParams(
            dimension_semantics=("arbitrary", "arbitrary"), vmem_limit_bytes=VMEM_LIMIT_BYTES),
        name=f"mixer_l{layer}",
    )(p, *consts)


def _ffn_kernel(x_ref, g_ref, win_ref, wout_ref, gf_ref, o_ref, *, final):
    x = x_ref[...]
    xn = _mx(_rms_norm(x, g_ref[...]))
    acc = jnp.zeros(x.shape, _F32)
    for j in range(D_FF // FF_BLOCK):
        lo = j * FF_BLOCK
        gate = jnp.dot(xn, win_ref[:, lo:lo + FF_BLOCK], preferred_element_type=_F32)
        up = jnp.dot(xn, win_ref[:, D_FF + lo:D_FF + lo + FF_BLOCK], preferred_element_type=_F32)
        act = _mx(gate * _sigmoid(gate) * up)
        acc = acc + jnp.dot(act, wout_ref[lo:lo + FF_BLOCK, :], preferred_element_type=_F32)
    y = x + 0.5 * acc
    if final:
        y = _rms_norm(y, gf_ref[...])
    o_ref[...] = y


def _ffn_call(x2, g, w_in, w_out, g_final, final):
    n, d = x2.shape
    tm = ROW_TILE
    return pl.pallas_call(
        functools.partial(_ffn_kernel, final=final),
        grid=(n // tm,),
        in_specs=[pl.BlockSpec((tm, d), lambda i: (i, 0)), _const_spec(g.shape), _const_spec(w_in.shape),
                  _const_spec(w_out.shape), _const_spec(g_final.shape)],
        out_specs=pl.BlockSpec((tm, d), lambda i: (i, 0)),
        out_shape=jax.ShapeDtypeStruct((n, d), _F32),
        compiler_params=pltpu.CompilerParams(
            dimension_semantics=("arbitrary",), vmem_limit_bytes=VMEM_LIMIT_BYTES),
        name="ffn",
    )(x2, g, w_in, w_out, g_final)


def _inproj_kernel(x_ref, g_ref, w_ref, o_ref):
    xn = _mx(_rms_norm(x_ref[...], g_ref[...]))
    o_ref[...] = jnp.dot(xn, w_ref[...], preferred_element_type=_F32)


def _inproj_call(x2, g, w):
    n, d = x2.shape
    tm = ROW_TILE
    return pl.pallas_call(
        _inproj_kernel,
        grid=(n // tm,),
        in_specs=[pl.BlockSpec((tm, d), lambda i: (i, 0)), _const_spec(g.shape), _const_spec(w.shape)],
        out_specs=pl.BlockSpec((tm, w.shape[1]), lambda i: (i, 0)),
        out_shape=jax.ShapeDtypeStruct((n, w.shape[1]), _F32),
        compiler_params=pltpu.CompilerParams(
            dimension_semantics=("arbitrary",), vmem_limit_bytes=VMEM_LIMIT_BYTES),
        name="inproj",
    )(x2, g, w)


def _outproj_kernel(x_ref, y_ref, w_ref, o_ref):
    o_ref[...] = x_ref[...] + jnp.dot(y_ref[...], w_ref[...], preferred_element_type=_F32)


def _outproj_call(x2, y2, w):
    n, d = x2.shape
    tm = ROW_TILE
    return pl.pallas_call(
        _outproj_kernel,
        grid=(n // tm,),
        in_specs=[pl.BlockSpec((tm, d), lambda i: (i, 0)), pl.BlockSpec((tm, y2.shape[1]), lambda i: (i, 0)),
                  _const_spec(w.shape)],
        out_specs=pl.BlockSpec((tm, d), lambda i: (i, 0)),
        out_shape=jax.ShapeDtypeStruct((n, d), _F32),
        compiler_params=pltpu.CompilerParams(
            dimension_semantics=("arbitrary",), vmem_limit_bytes=VMEM_LIMIT_BYTES),
        name="outproj",
    )(x2, y2, w)


def _pad_rows(w, rows):
    return jnp.pad(w, ((0, rows - w.shape[0]), (0, 0)))


def kernel(x, norm_ffn1, ffn1_w_in, ffn1_w_out, norm_mix, w_in, w_out, pool_w, pool_b, pool_scale,
           hgrn_lb_logits, hgrn_norm, rwkv_mu, rwkv_w0, rwkv_w2, rwkv_a0, rwkv_a2, rwkv_g2, rwkv_k_k,
           rwkv_k_a, rwkv_r_k, rwkv_ln_w, rwkv_ln_b, gla_w2, gla_b, gla_norm, norm_ffn2, ffn2_w_in,
           ffn2_w_out, norm_final):
    b, t, d = x.shape
    depth = norm_ffn1.shape[0]
    gw = GROUP_W
    assert d == D_MODEL and t % TIME_TILE == 0 and (b * t) % ROW_TILE == 0

    sel = jnp.asarray(_selection_matrix(), _MXU_DTYPE)
    tri = jnp.asarray(np.tril(np.ones((CHUNK, CHUNK), np.float32)), _MXU_DTYPE)
    head = np.arange(gw) // HEAD_DIM
    seg = jnp.asarray((head[:, None] == head[None, :]).astype(np.float32) / HEAD_DIM, _MXU_DTYPE)
    bdmask = jnp.asarray((head[:, None] == head[None, :]).astype(np.float32))
    gf = norm_final.reshape(1, d)

    h = x.reshape(b * t, d)
    for l in range(depth):
        h = _ffn_call(h, norm_ffn1[l].reshape(1, d), _mx(ffn1_w_in[l]), _mx(ffn1_w_out[l]), gf, False)

        w_in_p = _mx(jnp.pad(w_in[l], ((0, 0), (0, D_IN_PAD - D_IN))))
        p = _inproj_call(h, norm_mix[l].reshape(1, d), w_in_p)
        poolw = _mx(jnp.tile(pool_w[l].reshape(gw, POOL_CH), (1, len(POOL_WINDOWS))) * bdmask)
        w2p = _mx(_pad_rows(rwkv_w2[l], LANES))
        a2p = _mx(jnp.pad(rwkv_a2[l], ((RWKV_DECAY_LORA, 0), (0, 0))))
        vecs = jnp.stack([pool_b[l], pool_scale[l], hgrn_norm[l], rwkv_w0[l], rwkv_a0[l], rwkv_k_k[l],
                          rwkv_k_a[l], rwkv_r_k[l], rwkv_ln_w[l], rwkv_ln_b[l], gla_b[l], gla_norm[l]])
        consts = (sel, tri, seg, poolw, w2p, a2p, _mx(rwkv_g2[l]), _mx(_pad_rows(gla_w2[l], LANES)),
                  rwkv_mu[l].reshape(1, 4 * gw), _pad_rows(vecs, _N_VEC), hgrn_lb_logits)
        y = _mixer_call(p.reshape(b, t, D_IN_PAD), consts, l)
        h = _outproj_call(h, y.reshape(b * t, d), _mx(w_out[l]))

        h = _ffn_call(h, norm_ffn2[l].reshape(1, d), _mx(ffn2_w_in[l]), _mx(ffn2_w_out[l]), gf,
                      l == depth - 1)
    return h.reshape(b, t, d)
```

```python
import functools

import numpy as np
import jax
import jax.numpy as jnp
from jax import lax
from jax.experimental import pallas as pl
from jax.experimental.pallas import tpu as pltpu

D_MODEL = 1024
GROUP_W = 256
HEAD_DIM = 64
N_HEADS = GROUP_W // HEAD_DIM
HEAD_SHIFT = HEAD_DIM.bit_length() - 1
QK_SCALE = HEAD_DIM ** -0.5
POOL_WINDOWS = (2, 4, 8, 16)
POOL_CH = GROUP_W // len(POOL_WINDOWS)
MAX_WINDOW = 16
RWKV_DECAY_LORA = 64
RWKV_A_LORA = 64
RWKV_GATE_LORA = 128
RWKV_GN_EPS = 64e-5
GLA_GATE_LORA = 16
GLA_GATE_TAU = 16.0
D_IN = 13 * GROUP_W + GLA_GATE_LORA
D_FF = 2816
NORM_EPS = 1e-6
GATE_FLOOR = 1e-30

LANES = 128
VMEM_LIMIT_BYTES = 56 * 1024 * 1024

D_IN_PAD = 13 * GROUP_W + LANES
CHUNK = 64
LEVELS = CHUNK.bit_length() - 1
TIME_TILE = 256
CHUNK_UNROLL = 2
ROW_TILE = 512
FF_BLOCK = D_FF // 2

_P_POOL = 0
_P_HG = GROUP_W
_P_RW = 5 * GROUP_W
_P_GLA = 9 * GROUP_W

_F_HQ, _F_HK, _F_HV, _F_HGD = 0, 1, 2, 3
_F_RR, _F_RK, _F_RV, _F_RA, _F_RB, _F_RW = 4, 5, 6, 7, 8, 9
_F_GQ, _F_GK, _F_GV, _F_GGD = 10, 11, 12, 13
_N_FEAT = 14

(_V_POOL_B, _V_POOL_SCALE, _V_HGRN_NORM, _V_W0, _V_A0, _V_KK, _V_KA, _V_RK, _V_LNW, _V_LNB,
 _V_GLA_B, _V_GLA_NORM) = range(12)
_N_VEC = 16

_MXU_DTYPE = jnp.bfloat16
_F32 = jnp.float32


def _mx(a):
    return a.astype(_MXU_DTYPE)


def _dot(a, b):
    return jnp.dot(_mx(a), _mx(b), preferred_element_type=_F32)


def _dot_nt(a, b):
    return lax.dot_general(_mx(a), _mx(b), (((1,), (1,)), ((), ())), preferred_element_type=_F32)


def _dot_tn(a, b):
    return lax.dot_general(_mx(a), _mx(b), (((0,), (0,)), ((), ())), preferred_element_type=_F32)


def _split(x, parts):
    out = []
    r = x
    for _ in range(parts - 1):
        h = r.astype(_MXU_DTYPE)
        out.append(h)
        r = r - h.astype(_F32)
    out.append(r.astype(_MXU_DTYPE))
    return out


def _dot_exact_lhs(m, x, parts=3):
    m = _mx(m)
    acc = None
    for piece in _split(x, parts):
        t = jnp.dot(m, piece, preferred_element_type=_F32)
        acc = t if acc is None else acc + t
    return acc


def _dot_exact_rhs(x, m, parts=2):
    m = _mx(m)
    acc = None
    for piece in _split(x, parts):
        t = jnp.dot(piece, m, preferred_element_type=_F32)
        acc = t if acc is None else acc + t
    return acc


def _rms_norm(x, g, eps=NORM_EPS):
    return x * lax.rsqrt(jnp.mean(x * x, axis=-1, keepdims=True) + eps) * g


def _sigmoid(x):
    return 1.0 / (1.0 + jnp.exp(-x))


def _softplus(x):
    return jnp.maximum(x, 0.0) + jnp.log(1.0 + jnp.exp(-jnp.abs(x)))


def _head_masks():
    lane = lax.broadcasted_iota(jnp.int32, (1, GROUP_W), 1)
    return [(lax.shift_right_logical(lane, HEAD_SHIFT) == h).astype(_F32) for h in range(N_HEADS)]


def _stack_heads(x, hms):
    return jnp.concatenate([x * hm for hm in hms], axis=0)


def _mid_rows(b, n):
    c, gw = b.shape
    if n == c:
        return jnp.broadcast_to(b[n // 2 - 1:n // 2, :], b.shape)
    b3 = b.reshape(c // n, n, gw)
    return jnp.broadcast_to(b3[:, n // 2 - 1:n // 2, :], b3.shape).reshape(c, gw)


def _gla_chunk(q, k, v, g, sel, hms, xr, bd):
    c = CHUNK
    m = _dot_exact_lhs(sel, g)
    yield
    e2 = m[0:c]
    b = m[c:2 * c]
    f_b = jnp.exp(b)
    kv = jnp.where(bd, _dot_tn(v, k * jnp.exp(b[c - 1:c, :] - b)), 0.0)
    yield
    t_idx = lax.broadcasted_iota(jnp.int32, (c, GROUP_W), 0)
    sc = jnp.where(xr < 1, _dot_nt(_stack_heads(q, hms), k), 0.0)
    yield
    for j in range(1, LEVELS + 1):
        n = 1 << j
        upper = (t_idx & (n - 1)) >= n // 2
        if j == 1:
            ex = jnp.where(upper, g, 0.0)
        elif j == 2:
            ex = e2
        else:
            bm = _mid_rows(b, n)
            ex = jnp.where(upper, b - bm, bm - b)
        f_j = jnp.exp(ex)
        qt = jnp.where(upper, q * f_j, 0.0)
        kt = jnp.where(upper, 0.0, k * f_j)
        s_j = _dot_nt(_stack_heads(qt, hms), kt)
        yield
        sc = sc + (s_j if j == LEVELS else jnp.where(xr < n, s_j, 0.0))
    o = _dot(sc[0:c], v * hms[0])
    for h in range(1, N_HEADS):
        o = o + _dot(sc[h * c:(h + 1) * c], v * hms[h])
    yield
    yield o, q * f_b, f_b[c - 1:c, :], kv


def _gla_state_step(pre, st):
    o_intra, qb, e_last, kv = pre
    return o_intra + _dot_nt(qb, st), st * e_last + kv


def _rwkv_chunk(r, k, v, al, be, lw, tri, hms, bd_c, bd_g, eye_g):
    c = CHUNK
    hc = N_HEADS * c
    gw = GROUP_W

    def bdiag(x_row):
        return jnp.where(bd_c, jnp.concatenate([x_row] * N_HEADS, axis=0), 0.0)

    cw = _dot_exact_lhs(tri, lw)
    yield
    cw_last = cw[c - 1:c, :]
    e_neg = jnp.exp(-cw)
    e_rem = jnp.exp(cw_last - cw)
    a_t = al * jnp.exp(cw - lw)
    r_t = r * jnp.exp(cw)
    prod = _dot_nt(jnp.concatenate([a_t, r_t], axis=0),
                   jnp.concatenate([_stack_heads(be * e_neg, hms), _stack_heads(k * e_neg, hms)], axis=0))
    yield
    t_r = lax.broadcasted_iota(jnp.int32, (c, hc), 0)
    s_r = lax.broadcasted_iota(jnp.int32, (c, hc), 1) & (c - 1)
    l_row = jnp.where(s_r < t_r, prod[0:c, 0:hc], 0.0)
    ak_row = jnp.where(s_r < t_r, prod[0:c, hc:2 * hc], 0.0)
    rb_row = jnp.where(s_r <= t_r, prod[c:2 * c, 0:hc], 0.0)
    rk_row = jnp.where(s_r <= t_r, prod[c:2 * c, hc:2 * hc], 0.0)

    t_row = jnp.where(s_r == t_r, 1.0, 0.0) - l_row
    p_row = _dot(l_row, bdiag(l_row))
    yield
    for i in range(LEVELS - 1):
        last = i == LEVELS - 2
        res = _dot(t_row if last else jnp.concatenate([p_row, t_row], axis=0), bdiag(p_row))
        yield
        if last:
            t_row = t_row + res
        else:
            p_row = res[0:c]
            t_row = t_row + res[c:2 * c]

    v_st = _stack_heads(v, hms)
    mv = _dot(ak_row, v_st)
    yield
    tm = _dot(t_row, jnp.concatenate([_stack_heads(a_t, hms), _stack_heads(mv, hms)], axis=1))
    yield
    ta = tm[:, 0:gw]
    uv = tm[:, gw:2 * gw]
    qe = r_t - _dot(rb_row, _stack_heads(ta, hms))
    yield
    ol = _dot(jnp.concatenate([rk_row, -rb_row], axis=1),
              jnp.concatenate([v_st, _stack_heads(uv, hms)], axis=0))
    yield
    b_r = be * e_rem
    gm = jnp.where(eye_g, jnp.exp(cw_last), 0.0) - jnp.where(bd_g, _dot_tn(b_r, ta), 0.0)
    yield
    hm = jnp.where(bd_g, _dot_tn(jnp.concatenate([k * e_rem, -b_r], axis=0),
                                 jnp.concatenate([v, uv], axis=0)), 0.0)
    yield
    yield gm, hm, qe, ol


def _rwkv_state_step(pre, p):
    gm, hm, qe, ol = pre
    gw = GROUP_W
    res = _dot(jnp.concatenate([gm, qe], axis=0), p)
    return res[gw:] + ol, res[0:gw] + hm


def _interleave(gens):
    results = [None] * len(gens)
    live = list(range(len(gens)))
    while live:
        for i in list(live):
            out = next(gens[i])
            if out is not None:
                results[i] = out
                live.remove(i)
    return results


def _mixer_kernel(p_ref, sel_ref, tri_ref, seg_ref, poolw_ref, w2_ref, a2_ref, g2_ref, glaw2_ref,
                  mu_ref, vec_ref, lb_ref, y_ref,
                  feat, osc, pool_prev, rw_prev, hg_st, rw_p, gl_st, *, layer):
    tb = TIME_TILE
    gw = GROUP_W
    ti = pl.program_id(1)

    @pl.when(ti == 0)
    def _():
        pool_prev[...] = jnp.zeros_like(pool_prev)
        rw_prev[...] = jnp.zeros_like(rw_prev)
        hg_st[...] = jnp.zeros_like(hg_st)
        rw_p[...] = jnp.zeros_like(rw_p)
        gl_st[...] = jnp.zeros_like(gl_st)

    def vec(i):
        return vec_ref[i:i + 1, :]

    def pcol(base, i, n=1):
        return p_ref[0, :, base + i * gw: base + (i + n) * gw]

    seg = seg_ref[...]
    hms = _head_masks()

    p_pool = pcol(_P_POOL, 0)
    ext = jnp.concatenate([pool_prev[...], p_pool], axis=0)
    pool_prev[...] = p_pool[tb - MAX_WINDOW:, :]
    lane = lax.broadcasted_iota(jnp.int32, (tb, gw), 1)
    grp = lax.shift_right_logical(lane, HEAD_SHIFT)
    tpos = lax.broadcasted_iota(jnp.int32, (tb, gw), 0) + ti * tb
    win = ext
    win_sum = jnp.zeros((tb, gw), _F32)
    count = jnp.zeros((tb, gw), _F32)
    for gi, w in enumerate(POOL_WINDOWS):
        sh = w // 2
        while sh < w:
            win = win + pltpu.roll(win, sh, 0)
            sh *= 2
        win_sum = jnp.where(grp == gi, win[MAX_WINDOW:, :], win_sum)
        count = jnp.where(grp == gi, jnp.minimum(tpos + 1, w).astype(_F32), count)
    pooled = win_sum / count - p_pool
    y_pool = (_dot(pooled, poolw_ref[...]) + vec(_V_POOL_B)) * vec(_V_POOL_SCALE)
    y_ref[0, :, 0:gw] = y_pool.astype(y_ref.dtype)

    sm = jnp.exp(lb_ref[...] - jnp.max(lb_ref[...], axis=0, keepdims=True))
    sm = sm / jnp.sum(sm, axis=0, keepdims=True)
    lb = jnp.sum(sm[0:layer + 1], axis=0, keepdims=True) - sm[0:1]
    hq = pcol(_P_HG, 0)
    f = lb + (1.0 - lb) * _sigmoid(pcol(_P_HG, 1))
    feat[:, _F_HQ * gw:(_F_HQ + 1) * gw] = hq * _sigmoid(hq) * QK_SCALE
    feat[:, _F_HK * gw:(_F_HK + 1) * gw] = 1.0 - f
    feat[:, _F_HV * gw:(_F_HV + 1) * gw] = pcol(_P_HG, 2)
    feat[:, _F_HGD * gw:(_F_HGD + 1) * gw] = jnp.log(jnp.maximum(f, GATE_FLOOR))

    p_rw = pcol(_P_RW, 0, 4)
    row0 = lax.broadcasted_iota(jnp.int32, p_rw.shape, 0) == 0
    prev = jnp.where(row0, rw_prev[...], pltpu.roll(p_rw, 1, 0))
    rw_prev[...] = p_rw[tb - 1:tb, :]
    p_rw = p_rw + (prev - p_rw) * mu_ref[...]
    rr = p_rw[:, 0:gw]
    rk = p_rw[:, gw:2 * gw]
    rv = p_rw[:, 2 * gw:3 * gw]
    xwa = p_rw[:, 3 * gw:3 * gw + LANES]
    xg = p_rw[:, 3 * gw + LANES:4 * gw]
    w_log = -_softplus(-(vec(_V_W0) + _dot(jnp.tanh(xwa), w2_ref[...]))) - 0.5
    a = _sigmoid(vec(_V_A0) + _dot(xwa, a2_ref[...]))
    g_r = _dot(_sigmoid(xg), g2_ref[...])
    kk = rk * vec(_V_KK)
    kk = kk / jnp.maximum(jnp.sqrt(_dot_exact_rhs(kk * kk, seg) * HEAD_DIM), 1e-12)
    rk = rk * (1.0 + (a - 1.0) * vec(_V_KA))
    feat[:, _F_RR * gw:(_F_RR + 1) * gw] = rr
    feat[:, _F_RK * gw:(_F_RK + 1) * gw] = rk
    feat[:, _F_RV * gw:(_F_RV + 1) * gw] = rv
    feat[:, _F_RA * gw:(_F_RA + 1) * gw] = kk
    feat[:, _F_RB * gw:(_F_RB + 1) * gw] = kk * a
    feat[:, _F_RW * gw:(_F_RW + 1) * gw] = -jnp.exp(w_log)
    bonus = _dot_exact_rhs(rr * rk * vec(_V_RK), seg) * HEAD_DIM * rv

    ga = p_ref[0, :, _P_GLA + 4 * gw:_P_GLA + 4 * gw + LANES]
    z = _dot(ga, glaw2_ref[...]) + vec(_V_GLA_B)
    feat[:, _F_GQ * gw:(_F_GQ + 1) * gw] = pcol(_P_GLA, 0) * QK_SCALE
    feat[:, _F_GK * gw:(_F_GK + 1) * gw] = pcol(_P_GLA, 1)
    feat[:, _F_GV * gw:(_F_GV + 1) * gw] = pcol(_P_GLA, 2)
    feat[:, _F_GGD * gw:(_F_GGD + 1) * gw] = -_softplus(-z) / GLA_GATE_TAU

    c = CHUNK
    hc = N_HEADS * c
    xr = (lax.broadcasted_iota(jnp.int32, (hc, c), 0) & (c - 1)) ^ lax.broadcasted_iota(jnp.int32, (hc, c), 1)
    bd_c = (lax.shift_right_logical(lax.broadcasted_iota(jnp.int32, (hc, hc), 0), LEVELS)
            == lax.shift_right_logical(lax.broadcasted_iota(jnp.int32, (hc, hc), 1), LEVELS))
    row_g = lax.broadcasted_iota(jnp.int32, (gw, gw), 0)
    col_g = lax.broadcasted_iota(jnp.int32, (gw, gw), 1)
    bd_g = lax.shift_right_logical(row_g, HEAD_SHIFT) == lax.shift_right_logical(col_g, HEAD_SHIFT)
    eye_g = row_g == col_g
    sel = sel_ref[...]
    tri = tri_ref[...]

    def chunk_body(ci, carry):
        gens = []
        for u in range(CHUNK_UNROLL):
            rows = pl.ds(pl.multiple_of((ci * CHUNK_UNROLL + u) * c, c), c)

            def ft(i, rows=rows):
                return feat[rows, i * gw:(i + 1) * gw]

            gens += [
                _rwkv_chunk(ft(_F_RR), ft(_F_RK), ft(_F_RV), ft(_F_RA), ft(_F_RB), ft(_F_RW),
                            tri, hms, bd_c, bd_g, eye_g),
                _gla_chunk(ft(_F_HQ), ft(_F_HK), ft(_F_HV), ft(_F_HGD), sel, hms, xr, bd_g),
                _gla_chunk(ft(_F_GQ), ft(_F_GK), ft(_F_GV), ft(_F_GGD), sel, hms, xr, bd_g)]
        pre = _interleave(gens)
        p_rw, st_h, st_g = rw_p[...], hg_st[...], gl_st[...]
        for u in range(CHUNK_UNROLL):
            rows = pl.ds(pl.multiple_of((ci * CHUNK_UNROLL + u) * c, c), c)
            o_r, p_rw = _rwkv_state_step(pre[3 * u], p_rw)
            o_h, st_h = _gla_state_step(pre[3 * u + 1], st_h)
            o_g, st_g = _gla_state_step(pre[3 * u + 2], st_g)
            osc[rows, 0:gw] = o_h
            osc[rows, gw:2 * gw] = o_r
            osc[rows, 2 * gw:3 * gw] = o_g
        rw_p[...] = p_rw
        hg_st[...] = st_h
        gl_st[...] = st_g
        return carry

    lax.fori_loop(0, tb // (c * CHUNK_UNROLL), chunk_body, 0)

    o_h = osc[:, 0:gw]
    y_hg = _rms_norm(o_h, vec(_V_HGRN_NORM)) * _sigmoid(pcol(_P_HG, 3))
    y_ref[0, :, gw:2 * gw] = y_hg.astype(y_ref.dtype)

    o_r = osc[:, gw:2 * gw]
    mean = _dot_exact_rhs(o_r, seg)
    cen = o_r - mean
    var = _dot_exact_rhs(cen * cen, seg)
    gn = cen * lax.rsqrt(var + RWKV_GN_EPS) * vec(_V_LNW) + vec(_V_LNB)
    y_ref[0, :, 2 * gw:3 * gw] = ((gn + bonus) * g_r).astype(y_ref.dtype)

    o_g = osc[:, 2 * gw:3 * gw]
    ms = _dot_exact_rhs(o_g * o_g, seg)
    gg = pcol(_P_GLA, 3)
    y_gl = o_g * lax.rsqrt(ms + NORM_EPS) * vec(_V_GLA_NORM) * (gg * _sigmoid(gg))
    y_ref[0, :, 3 * gw:4 * gw] = y_gl.astype(y_ref.dtype)


def _selection_matrix():
    c = CHUNK
    m = np.zeros((2 * c, c), np.float32)
    for t in range(c):
        mid = (t // 4) * 4 + 1
        if t % 4 >= 2:
            m[t, mid + 1:t + 1] = 1.0
        else:
            m[t, t + 1:mid + 1] = 1.0
        m[c + t, :t + 1] = 1.0
    return m


def _const_spec(shape):
    nd = len(shape)
    return pl.BlockSpec(shape, lambda *_: (0,) * nd, pipeline_mode=pl.Buffered(1))


def _mixer_call(p, consts, layer):
    b, t, _ = p.shape
    tb = TIME_TILE
    gw = GROUP_W
    in_specs = [pl.BlockSpec((1, tb, D_IN_PAD), lambda bi, ti: (bi, ti, 0))]
    in_specs += [_const_spec(a.shape) for a in consts]
    return pl.pallas_call(
        functools.partial(_mixer_kernel, layer=layer),
        grid=(b, t // tb),
        in_specs=in_specs,
        out_specs=pl.BlockSpec((1, tb, D_MODEL), lambda bi, ti: (bi, ti, 0)),
        out_shape=jax.ShapeDtypeStruct((b, t, D_MODEL), _MXU_DTYPE),
        scratch_shapes=[
            pltpu.VMEM((tb, _N_FEAT * gw), _F32),
            pltpu.VMEM((tb, 3 * gw), _F32),
            pltpu.VMEM((MAX_WINDOW, gw), _F32),
            pltpu.VMEM((1, 4 * gw), _F32),
            pltpu.VMEM((gw, gw), _F32),
            pltpu.VMEM((gw, gw), _F32),
            pltpu.VMEM((gw, gw), _F32),
        ],
        compiler_params=pltpu.CompilerParams(
            dimension_semantics=("arbitrary", "arbitrary"), vmem_limit_bytes=VMEM_LIMIT_BYTES),
        name=f"mixer_l{layer}",
    )(p, *consts)


def _ffn_kernel(x_ref, g_ref, win_ref, wout_ref, gf_ref, o_ref, *, final):
    x = x_ref[...]
    xn = _mx(_rms_norm(x, g_ref[...]))
    acc = jnp.zeros(x.shape, _F32)
    for j in range(D_FF // FF_BLOCK):
        lo = j * FF_BLOCK
        gate = jnp.dot(xn, win_ref[:, lo:lo + FF_BLOCK], preferred_element_type=_F32)
        up = jnp.dot(xn, win_ref[:, D_FF + lo:D_FF + lo + FF_BLOCK], preferred_element_type=_F32)
        act = _mx(gate * _sigmoid(gate) * up)
        acc = acc + jnp.dot(act, wout_ref[lo:lo + FF_BLOCK, :], preferred_element_type=_F32)
    y = x + 0.5 * acc
    if final:
        y = _rms_norm(y, gf_ref[...])
    o_ref[...] = y


def _ffn_call(x2, g, w_in, w_out, g_final, final):
    n, d = x2.shape
    tm = ROW_TILE
    return pl.pallas_call(
        functools.partial(_ffn_kernel, final=final),
        grid=(n // tm,),
        in_specs=[pl.BlockSpec((tm, d), lambda i: (i, 0)), _const_spec(g.shape), _const_spec(w_in.shape),
                  _const_spec(w_out.shape), _const_spec(g_final.shape)],
        out_specs=pl.BlockSpec((tm, d), lambda i: (i, 0)),
        out_shape=jax.ShapeDtypeStruct((n, d), _F32),
        compiler_params=pltpu.CompilerParams(
            dimension_semantics=("arbitrary",), vmem_limit_bytes=VMEM_LIMIT_BYTES),
        name="ffn",
    )(x2, g, w_in, w_out, g_final)


def _inproj_kernel(x_ref, g_ref, w_ref, o_ref):
    xn = _mx(_rms_norm(x_ref[...], g_ref[...]))
    o_ref[...] = jnp.dot(xn, w_ref[...], preferred_element_type=_F32)


def _inproj_call(x2, g, w):
    n, d = x2.shape
    tm = ROW_TILE
    return pl.pallas_call(
        _inproj_kernel,
        grid=(n // tm,),
        in_specs=[pl.BlockSpec((tm, d), lambda i: (i, 0)), _const_spec(g.shape), _const_spec(w.shape)],
        out_specs=pl.BlockSpec((tm, w.shape[1]), lambda i: (i, 0)),
        out_shape=jax.ShapeDtypeStruct((n, w.shape[1]), _F32),
        compiler_params=pltpu.CompilerParams(
            dimension_semantics=("arbitrary",), vmem_limit_bytes=VMEM_LIMIT_BYTES),
        name="inproj",
    )(x2, g, w)


def _outproj_kernel(x_ref, y_ref, w_ref, o_ref):
    o_ref[...] = x_ref[...] + jnp.dot(y_ref[...], w_ref[...], preferred_element_type=_F32)


def _outproj_call(x2, y2, w):
    n, d = x2.shape
    tm = ROW_TILE
    return pl.pallas_call(
        _outproj_kernel,
        grid=(n // tm,),
        in_specs=[pl.BlockSpec((tm, d), lambda i: (i, 0)), pl.BlockSpec((tm, y2.shape[1]), lambda i: (i, 0)),
                  _const_spec(w.shape)],
        out_specs=pl.BlockSpec((tm, d), lambda i: (i, 0)),
        out_shape=jax.ShapeDtypeStruct((n, d), _F32),
        compiler_params=pltpu.CompilerParams(
            dimension_semantics=("arbitrary",), vmem_limit_bytes=VMEM_LIMIT_BYTES),
        name="outproj",
    )(x2, y2, w)


def _pad_rows(w, rows):
    return jnp.pad(w, ((0, rows - w.shape[0]), (0, 0)))


def kernel(x, norm_ffn1, ffn1_w_in, ffn1_w_out, norm_mix, w_in, w_out, pool_w, pool_b, pool_scale,
           hgrn_lb_logits, hgrn_norm, rwkv_mu, rwkv_w0, rwkv_w2, rwkv_a0, rwkv_a2, rwkv_g2, rwkv_k_k,
           rwkv_k_a, rwkv_r_k, rwkv_ln_w, rwkv_ln_b, gla_w2, gla_b, gla_norm, norm_ffn2, ffn2_w_in,
           ffn2_w_out, norm_final):
    b, t, d = x.shape
    depth = norm_ffn1.shape[0]
    gw = GROUP_W
    assert d == D_MODEL and t % TIME_TILE == 0 and (b * t) % ROW_TILE == 0
    assert TIME_TILE % (CHUNK * CHUNK_UNROLL) == 0

    sel = jnp.asarray(_selection_matrix(), _MXU_DTYPE)
    tri = jnp.asarray(np.tril(np.ones((CHUNK, CHUNK), np.float32)), _MXU_DTYPE)
    head = np.arange(gw) // HEAD_DIM
    seg = jnp.asarray((head[:, None] == head[None, :]).astype(np.float32) / HEAD_DIM, _MXU_DTYPE)
    bdmask = jnp.asarray((head[:, None] == head[None, :]).astype(np.float32))
    gf = norm_final.reshape(1, d)

    h = x.reshape(b * t, d)
    for l in range(depth):
        h = _ffn_call(h, norm_ffn1[l].reshape(1, d), _mx(ffn1_w_in[l]), _mx(ffn1_w_out[l]), gf, False)

        w_in_p = _mx(jnp.pad(w_in[l], ((0, 0), (0, D_IN_PAD - D_IN))))
        p = _inproj_call(h, norm_mix[l].reshape(1, d), w_in_p)
        poolw = _mx(jnp.tile(pool_w[l].reshape(gw, POOL_CH), (1, len(POOL_WINDOWS))) * bdmask)
        w2p = _mx(_pad_rows(rwkv_w2[l], LANES))
        a2p = _mx(jnp.pad(rwkv_a2[l], ((RWKV_DECAY_LORA, 0), (0, 0))))
        vecs = jnp.stack([pool_b[l], pool_scale[l], hgrn_norm[l], rwkv_w0[l], rwkv_a0[l], rwkv_k_k[l],
                          rwkv_k_a[l], rwkv_r_k[l], rwkv_ln_w[l], rwkv_ln_b[l], gla_b[l], gla_norm[l]])
        consts = (sel, tri, seg, poolw, w2p, a2p, _mx(rwkv_g2[l]), _mx(_pad_rows(gla_w2[l], LANES)),
                  rwkv_mu[l].reshape(1, 4 * gw), _pad_rows(vecs, _N_VEC), hgrn_lb_logits)
        y = _mixer_call(p.reshape(b, t, D_IN_PAD), consts, l)
        h = _outproj_call(h, y.reshape(b * t, d), _mx(w_out[l]))

        h = _ffn_call(h, norm_ffn2[l].reshape(1, d), _mx(ffn2_w_in[l]), _mx(ffn2_w_out[l]), gf,
                      l == depth - 1)
    return h.reshape(b, t, d)
```

```python
import functools

import numpy as np
import jax
import jax.numpy as jnp
from jax import lax
from jax.experimental import pallas as pl
from jax.experimental.pallas import tpu as pltpu

D_MODEL = 1024
GROUP_W = 256
HEAD_DIM = 64
N_HEADS = GROUP_W // HEAD_DIM
HEAD_SHIFT = HEAD_DIM.bit_length() - 1
QK_SCALE = HEAD_DIM ** -0.5
POOL_WINDOWS = (2, 4, 8, 16)
POOL_CH = GROUP_W // len(POOL_WINDOWS)
MAX_WINDOW = 16
RWKV_DECAY_LORA = 64
RWKV_A_LORA = 64
RWKV_GATE_LORA = 128
RWKV_GN_EPS = 64e-5
GLA_GATE_LORA = 16
GLA_GATE_TAU = 16.0
D_IN = 13 * GROUP_W + GLA_GATE_LORA
D_FF = 2816
NORM_EPS = 1e-6
GATE_FLOOR = 1e-30

LANES = 128
VMEM_LIMIT_BYTES = 56 * 1024 * 1024

D_IN_PAD = 13 * GROUP_W + LANES
CHUNK = 64
LEVELS = CHUNK.bit_length() - 1
TIME_TILE = 256
CHUNK_UNROLL = 2
ROW_TILE = 512
MXU_TILE = 256
FF_BLOCKS = (1536, 1280)

_P_POOL = 0
_P_HG = GROUP_W
_P_RW = 5 * GROUP_W
_P_GLA = 9 * GROUP_W

_F_HQ, _F_HK, _F_HV, _F_HGD = 0, 1, 2, 3
_F_RR, _F_RK, _F_RV, _F_RA, _F_RB, _F_RW = 4, 5, 6, 7, 8, 9
_F_GQ, _F_GK, _F_GV, _F_GGD = 10, 11, 12, 13
_N_FEAT = 14

(_V_POOL_B, _V_POOL_SCALE, _V_HGRN_NORM, _V_W0, _V_A0, _V_KK, _V_KA, _V_RK, _V_LNW, _V_LNB,
 _V_GLA_B, _V_GLA_NORM) = range(12)
_N_VEC = 16

_MXU_DTYPE = jnp.bfloat16
_F32 = jnp.float32


def _mx(a):
    return a.astype(_MXU_DTYPE)


def _dot(a, b):
    return jnp.dot(_mx(a), _mx(b), preferred_element_type=_F32)


def _dot_nt(a, b):
    return lax.dot_general(_mx(a), _mx(b), (((1,), (1,)), ((), ())), preferred_element_type=_F32)


def _dot_tn(a, b):
    return lax.dot_general(_mx(a), _mx(b), (((0,), (0,)), ((), ())), preferred_element_type=_F32)


def _split(x, parts):
    out = []
    r = x
    for _ in range(parts - 1):
        h = r.astype(_MXU_DTYPE)
        out.append(h)
        r = r - h.astype(_F32)
    out.append(r.astype(_MXU_DTYPE))
    return out


def _dot_exact_lhs(m, x, parts=3):
    m = _mx(m)
    acc = None
    for piece in _split(x, parts):
        t = jnp.dot(m, piece, preferred_element_type=_F32)
        acc = t if acc is None else acc + t
    return acc


def _dot_exact_rhs(x, m, parts=2):
    m = _mx(m)
    acc = None
    for piece in _split(x, parts):
        t = jnp.dot(piece, m, preferred_element_type=_F32)
        acc = t if acc is None else acc + t
    return acc


def _rms_norm(x, g, eps=NORM_EPS):
    return x * lax.rsqrt(jnp.mean(x * x, axis=-1, keepdims=True) + eps) * g


def _sigmoid(x):
    return 0.5 * jnp.tanh(0.5 * x) + 0.5


def _softplus(x):
    return jnp.maximum(x, 0.0) + jnp.log(1.0 + jnp.exp(-jnp.abs(x)))


def _head_masks():
    lane = lax.broadcasted_iota(jnp.int32, (1, GROUP_W), 1)
    return [(lax.shift_right_logical(lane, HEAD_SHIFT) == h).astype(_MXU_DTYPE) for h in range(N_HEADS)]


def _stack_heads(x, hms):
    x = _mx(x)
    return jnp.concatenate([x * hm for hm in hms], axis=0)


def _mid_rows(b, n):
    c, gw = b.shape
    if n == c:
        return jnp.broadcast_to(b[n // 2 - 1:n // 2, :], b.shape)
    b3 = b.reshape(c // n, n, gw)
    return jnp.broadcast_to(b3[:, n // 2 - 1:n // 2, :], b3.shape).reshape(c, gw)


def _gla_chunk(q, k, v, g, sel, hms, xr, bd):
    c = CHUNK
    m = _dot_exact_lhs(sel, g)
    yield
    e2 = m[0:c]
    b = m[c:2 * c]
    f_b = jnp.exp(b)
    kv = jnp.where(bd, _dot_tn(v, k * jnp.exp(b[c - 1:c, :] - b)), 0.0)
    yield
    t_idx = lax.broadcasted_iota(jnp.int32, (c, GROUP_W), 0)
    sc = jnp.where(xr < 1, _dot_nt(_stack_heads(q, hms), k), 0.0)
    yield
    for j in range(1, LEVELS + 1):
        n = 1 << j
        upper = (t_idx & (n - 1)) >= n // 2
        if j == 1:
            ex = jnp.where(upper, g, 0.0)
        elif j == 2:
            ex = e2
        else:
            bm = _mid_rows(b, n)
            ex = jnp.where(upper, b - bm, bm - b)
        f_j = jnp.exp(ex)
        qt = jnp.where(upper, q * f_j, 0.0)
        kt = jnp.where(upper, 0.0, k * f_j)
        s_j = _dot_nt(_stack_heads(qt, hms), kt)
        yield
        sc = sc + (s_j if j == LEVELS else jnp.where(xr < n, s_j, 0.0))
    v_m = _mx(v)
    o = _dot(sc[0:c], v_m * hms[0])
    for h in range(1, N_HEADS):
        o = o + _dot(sc[h * c:(h + 1) * c], v_m * hms[h])
    yield
    yield o, q * f_b, f_b[c - 1:c, :], kv


def _gla_state_step(pre, st):
    o_intra, qb, e_last, kv = pre
    return o_intra + _dot_nt(qb, st), st * e_last + kv


def _rwkv_chunk(r, k, v, al, be, lw, tri, hms, bd_c, bd_g, eye_g):
    c = CHUNK
    hc = N_HEADS * c
    gw = GROUP_W

    def bdiag(x_row):
        return jnp.where(bd_c, jnp.concatenate([_mx(x_row)] * N_HEADS, axis=0), jnp.zeros((), _MXU_DTYPE))

    cw = _dot_exact_lhs(tri, lw)
    yield
    cw_last = cw[c - 1:c, :]
    e_neg = jnp.exp(-cw)
    e_rem = jnp.exp(cw_last - cw)
    a_t = al * jnp.exp(cw - lw)
    r_t = r * jnp.exp(cw)
    prod = _dot_nt(jnp.concatenate([a_t, r_t], axis=0),
                   jnp.concatenate([_stack_heads(be * e_neg, hms), _stack_heads(k * e_neg, hms)], axis=0))
    yield
    t_r = lax.broadcasted_iota(jnp.int32, (c, hc), 0)
    s_r = lax.broadcasted_iota(jnp.int32, (c, hc), 1) & (c - 1)
    l_row = jnp.where(s_r < t_r, prod[0:c, 0:hc], 0.0)
    ak_row = jnp.where(s_r < t_r, prod[0:c, hc:2 * hc], 0.0)
    rb_row = jnp.where(s_r <= t_r, prod[c:2 * c, 0:hc], 0.0)
    rk_row = jnp.where(s_r <= t_r, prod[c:2 * c, hc:2 * hc], 0.0)

    t_row = jnp.where(s_r == t_r, 1.0, 0.0) - l_row
    p_row = _dot(l_row, bdiag(l_row))
    yield
    for i in range(LEVELS - 1):
        last = i == LEVELS - 2
        res = _dot(t_row if last else jnp.concatenate([p_row, t_row], axis=0), bdiag(p_row))
        yield
        if last:
            t_row = t_row + res
        else:
            p_row = res[0:c]
            t_row = t_row + res[c:2 * c]

    v_st = _stack_heads(v, hms)
    mv = _dot(ak_row, v_st)
    yield
    tm = _dot(t_row, jnp.concatenate([_stack_heads(a_t, hms), _stack_heads(mv, hms)], axis=1))
    yield
    ta = tm[:, 0:gw]
    uv = tm[:, gw:2 * gw]
    qe = r_t - _dot(rb_row, _stack_heads(ta, hms))
    yield
    ol = _dot(jnp.concatenate([rk_row, -rb_row], axis=1),
              jnp.concatenate([v_st, _stack_heads(uv, hms)], axis=0))
    yield
    b_r = be * e_rem
    gm = jnp.where(eye_g, jnp.exp(cw_last), 0.0) - jnp.where(bd_g, _dot_tn(b_r, ta), 0.0)
    yield
    hm = jnp.where(bd_g, _dot_tn(jnp.concatenate([k * e_rem, -b_r], axis=0),
                                 jnp.concatenate([v, uv], axis=0)), 0.0)
    yield
    yield gm, hm, qe, ol


def _rwkv_state_step(pre, p):
    gm, hm, qe, ol = pre
    gw = GROUP_W
    res = _dot(jnp.concatenate([gm, qe], axis=0), p)
    return res[gw:] + ol, res[0:gw] + hm


def _interleave(gens):
    results = [None] * len(gens)
    live = list(range(len(gens)))
    while live:
        for i in list(live):
            out = next(gens[i])
            if out is not None:
                results[i] = out
                live.remove(i)
    return results


def _mixer_kernel(p_ref, sel_ref, tri_ref, seg_ref, poolw_ref, w2_ref, a2_ref, g2_ref, glaw2_ref,
                  mu_ref, vec_ref, lb_ref, y_ref,
                  feat, osc, pool_prev, rw_prev, hg_st, rw_p, gl_st, *, layer):
    tb = TIME_TILE
    gw = GROUP_W
    ti = pl.program_id(1)

    @pl.when(ti == 0)
    def _():
        pool_prev[...] = jnp.zeros_like(pool_prev)
        rw_prev[...] = jnp.zeros_like(rw_prev)
        hg_st[...] = jnp.zeros_like(hg_st)
        rw_p[...] = jnp.zeros_like(rw_p)
        gl_st[...] = jnp.zeros_like(gl_st)

    def vec(i):
        return vec_ref[i:i + 1, :]

    def pcol(base, i, n=1):
        return p_ref[0, :, base + i * gw: base + (i + n) * gw]

    seg = seg_ref[...]
    hms = _head_masks()

    p_pool = pcol(_P_POOL, 0)
    ext = jnp.concatenate([pool_prev[...], p_pool], axis=0)
    pool_prev[...] = p_pool[tb - MAX_WINDOW:, :]
    lane = lax.broadcasted_iota(jnp.int32, (tb, gw), 1)
    grp = lax.shift_right_logical(lane, HEAD_SHIFT)
    tpos = lax.broadcasted_iota(jnp.int32, (tb, gw), 0) + ti * tb
    win = ext
    win_sum = jnp.zeros((tb, gw), _F32)
    count = jnp.zeros((tb, gw), _F32)
    for gi, w in enumerate(POOL_WINDOWS):
        sh = w // 2
        while sh < w:
            win = win + pltpu.roll(win, sh, 0)
            sh *= 2
        win_sum = jnp.where(grp == gi, win[MAX_WINDOW:, :], win_sum)
        count = jnp.where(grp == gi, jnp.minimum(tpos + 1, w).astype(_F32), count)
    pooled = win_sum / count - p_pool
    y_pool = (_dot(pooled, poolw_ref[...]) + vec(_V_POOL_B)) * vec(_V_POOL_SCALE)
    y_ref[0, :, 0:gw] = y_pool.astype(y_ref.dtype)

    sm = jnp.exp(lb_ref[...] - jnp.max(lb_ref[...], axis=0, keepdims=True))
    sm = sm / jnp.sum(sm, axis=0, keepdims=True)
    lb = jnp.sum(sm[0:layer + 1], axis=0, keepdims=True) - sm[0:1]
    hq = pcol(_P_HG, 0)
    f = lb + (1.0 - lb) * _sigmoid(pcol(_P_HG, 1))
    feat[:, _F_HQ * gw:(_F_HQ + 1) * gw] = hq * _sigmoid(hq) * QK_SCALE
    feat[:, _F_HK * gw:(_F_HK + 1) * gw] = 1.0 - f
    feat[:, _F_HV * gw:(_F_HV + 1) * gw] = pcol(_P_HG, 2)
    feat[:, _F_HGD * gw:(_F_HGD + 1) * gw] = jnp.log(jnp.maximum(f, GATE_FLOOR))

    p_rw = pcol(_P_RW, 0, 4)
    row0 = lax.broadcasted_iota(jnp.int32, p_rw.shape, 0) == 0
    prev = jnp.where(row0, rw_prev[...], pltpu.roll(p_rw, 1, 0))
    rw_prev[...] = p_rw[tb - 1:tb, :]
    p_rw = p_rw + (prev - p_rw) * mu_ref[...]
    rr = p_rw[:, 0:gw]
    rk = p_rw[:, gw:2 * gw]
    rv = p_rw[:, 2 * gw:3 * gw]
    xwa = p_rw[:, 3 * gw:3 * gw + LANES]
    xg = p_rw[:, 3 * gw + LANES:4 * gw]
    w_log = -_softplus(-(vec(_V_W0) + _dot(jnp.tanh(xwa), w2_ref[...]))) - 0.5
    a = _sigmoid(vec(_V_A0) + _dot(xwa, a2_ref[...]))
    g_r = _dot(_sigmoid(xg), g2_ref[...])
    kk = rk * vec(_V_KK)
    kk = kk / jnp.maximum(jnp.sqrt(_dot_exact_rhs(kk * kk, seg) * HEAD_DIM), 1e-12)
    rk = rk * (1.0 + (a - 1.0) * vec(_V_KA))
    feat[:, _F_RR * gw:(_F_RR + 1) * gw] = rr
    feat[:, _F_RK * gw:(_F_RK + 1) * gw] = rk
    feat[:, _F_RV * gw:(_F_RV + 1) * gw] = rv
    feat[:, _F_RA * gw:(_F_RA + 1) * gw] = kk
    feat[:, _F_RB * gw:(_F_RB + 1) * gw] = kk * a
    feat[:, _F_RW * gw:(_F_RW + 1) * gw] = -jnp.exp(w_log)
    bonus = _dot_exact_rhs(rr * rk * vec(_V_RK), seg) * HEAD_DIM * rv

    ga = p_ref[0, :, _P_GLA + 4 * gw:_P_GLA + 4 * gw + LANES]
    z = _dot(ga, glaw2_ref[...]) + vec(_V_GLA_B)
    feat[:, _F_GQ * gw:(_F_GQ + 1) * gw] = pcol(_P_GLA, 0) * QK_SCALE
    feat[:, _F_GK * gw:(_F_GK + 1) * gw] = pcol(_P_GLA, 1)
    feat[:, _F_GV * gw:(_F_GV + 1) * gw] = pcol(_P_GLA, 2)
    feat[:, _F_GGD * gw:(_F_GGD + 1) * gw] = -_softplus(-z) / GLA_GATE_TAU

    c = CHUNK
    hc = N_HEADS * c
    xr = (lax.broadcasted_iota(jnp.int32, (hc, c), 0) & (c - 1)) ^ lax.broadcasted_iota(jnp.int32, (hc, c), 1)
    bd_c = (lax.shift_right_logical(lax.broadcasted_iota(jnp.int32, (hc, hc), 0), LEVELS)
            == lax.shift_right_logical(lax.broadcasted_iota(jnp.int32, (hc, hc), 1), LEVELS))
    row_g = lax.broadcasted_iota(jnp.int32, (gw, gw), 0)
    col_g = lax.broadcasted_iota(jnp.int32, (gw, gw), 1)
    bd_g = lax.shift_right_logical(row_g, HEAD_SHIFT) == lax.shift_right_logical(col_g, HEAD_SHIFT)
    eye_g = row_g == col_g
    sel = sel_ref[...]
    tri = tri_ref[...]

    def chunk_body(ci, carry):
        gens = []
        for u in range(CHUNK_UNROLL):
            rows = pl.ds(pl.multiple_of((ci * CHUNK_UNROLL + u) * c, c), c)

            def ft(i, rows=rows):
                return feat[rows, i * gw:(i + 1) * gw]

            gens += [
                _rwkv_chunk(ft(_F_RR), ft(_F_RK), ft(_F_RV), ft(_F_RA), ft(_F_RB), ft(_F_RW),
                            tri, hms, bd_c, bd_g, eye_g),
                _gla_chunk(ft(_F_HQ), ft(_F_HK), ft(_F_HV), ft(_F_HGD), sel, hms, xr, bd_g),
                _gla_chunk(ft(_F_GQ), ft(_F_GK), ft(_F_GV), ft(_F_GGD), sel, hms, xr, bd_g)]
        pre = _interleave(gens)
        p_rw, st_h, st_g = rw_p[...], hg_st[...], gl_st[...]
        for u in range(CHUNK_UNROLL):
            rows = pl.ds(pl.multiple_of((ci * CHUNK_UNROLL + u) * c, c), c)
            o_r, p_rw = _rwkv_state_step(pre[3 * u], p_rw)
            o_h, st_h = _gla_state_step(pre[3 * u + 1], st_h)
            o_g, st_g = _gla_state_step(pre[3 * u + 2], st_g)
            osc[rows, 0:gw] = o_h
            osc[rows, gw:2 * gw] = o_r
            osc[rows, 2 * gw:3 * gw] = o_g
        rw_p[...] = p_rw
        hg_st[...] = st_h
        gl_st[...] = st_g
        return carry

    lax.fori_loop(0, tb // (c * CHUNK_UNROLL), chunk_body, 0)

    o_h = osc[:, 0:gw]
    y_hg = _rms_norm(o_h, vec(_V_HGRN_NORM)) * _sigmoid(pcol(_P_HG, 3))
    y_ref[0, :, gw:2 * gw] = y_hg.astype(y_ref.dtype)

    o_r = osc[:, gw:2 * gw]
    mean = _dot_exact_rhs(o_r, seg)
    cen = o_r - mean
    var = _dot_exact_rhs(cen * cen, seg)
    gn = cen * lax.rsqrt(var + RWKV_GN_EPS) * vec(_V_LNW) + vec(_V_LNB)
    y_ref[0, :, 2 * gw:3 * gw] = ((gn + bonus) * g_r).astype(y_ref.dtype)

    o_g = osc[:, 2 * gw:3 * gw]
    ms = _dot_exact_rhs(o_g * o_g, seg)
    gg = pcol(_P_GLA, 3)
    y_gl = o_g * lax.rsqrt(ms + NORM_EPS) * vec(_V_GLA_NORM) * (gg * _sigmoid(gg))
    y_ref[0, :, 3 * gw:4 * gw] = y_gl.astype(y_ref.dtype)


def _selection_matrix():
    c = CHUNK
    m = np.zeros((2 * c, c), np.float32)
    for t in range(c):
        mid = (t // 4) * 4 + 1
        if t % 4 >= 2:
            m[t, mid + 1:t + 1] = 1.0
        else:
            m[t, t + 1:mid + 1] = 1.0
        m[c + t, :t + 1] = 1.0
    return m


def _const_spec(shape):
    nd = len(shape)
    return pl.BlockSpec(shape, lambda *_: (0,) * nd, pipeline_mode=pl.Buffered(1))


def _mixer_call(p, consts, layer):
    b, t, _ = p.shape
    tb = TIME_TILE
    gw = GROUP_W
    in_specs = [pl.BlockSpec((1, tb, D_IN_PAD), lambda bi, ti: (bi, ti, 0))]
    in_specs += [_const_spec(a.shape) for a in consts]
    return pl.pallas_call(
        functools.partial(_mixer_kernel, layer=layer),
        grid=(b, t // tb),
        in_specs=in_specs,
        out_specs=pl.BlockSpec((1, tb, D_MODEL), lambda bi, ti: (bi, ti, 0)),
        out_shape=jax.ShapeDtypeStruct((b, t, D_MODEL), _MXU_DTYPE),
        scratch_shapes=[
            pltpu.VMEM((tb, _N_FEAT * gw), _F32),
            pltpu.VMEM((tb, 3 * gw), _F32),
            pltpu.VMEM((MAX_WINDOW, gw), _F32),
            pltpu.VMEM((1, 4 * gw), _F32),
            pltpu.VMEM((gw, gw), _F32),
            pltpu.VMEM((gw, gw), _F32),
            pltpu.VMEM((gw, gw), _F32),
        ],
        compiler_params=pltpu.CompilerParams(
            dimension_semantics=("arbitrary", "arbitrary"), vmem_limit_bytes=VMEM_LIMIT_BYTES),
        name=f"mixer_l{layer}",
    )(p, *consts)


def _ffn_kernel(*refs, final, fused_proj):
    if fused_proj:
        x_ref, y_ref, wp_ref, g_ref, win_ref, wout_ref, gf_ref, o_ref = refs
        x = x_ref[...] + jnp.dot(y_ref[...], wp_ref[...], preferred_element_type=_F32)
    else:
        x_ref, g_ref, win_ref, wout_ref, gf_ref, o_ref = refs
        x = x_ref[...]
    xn = _mx(_rms_norm(x, g_ref[...]))
    acc = jnp.zeros(x.shape, _F32)
    lo = 0
    for width in FF_BLOCKS:
        gate = jnp.dot(xn, win_ref[:, lo:lo + width], preferred_element_type=_F32)
        up = jnp.dot(xn, win_ref[:, D_FF + lo:D_FF + lo + width], preferred_element_type=_F32)
        act = _mx(gate * _sigmoid(gate) * up)
        acc = acc + jnp.dot(act, wout_ref[lo:lo + width, :], preferred_element_type=_F32)
        lo += width
    y = x + 0.5 * acc
    if final:
        y = _rms_norm(y, gf_ref[...])
    o_ref[...] = y


def _ffn_call(x2, g, w_in, w_out, g_final, final, proj=None):
    n, d = x2.shape
    tm = ROW_TILE
    row_spec = pl.BlockSpec((tm, d), lambda i: (i, 0))
    args = [x2]
    in_specs = [row_spec]
    if proj is not None:
        y2, w_proj = proj
        args += [y2, w_proj]
        in_specs += [pl.BlockSpec((tm, y2.shape[1]), lambda i: (i, 0)), _const_spec(w_proj.shape)]
    args += [g, w_in, w_out, g_final]
    in_specs += [_const_spec(a.shape) for a in (g, w_in, w_out, g_final)]
    return pl.pallas_call(
        functools.partial(_ffn_kernel, final=final, fused_proj=proj is not None),
        grid=(n // tm,),
        in_specs=in_specs,
        out_specs=row_spec,
        out_shape=jax.ShapeDtypeStruct((n, d), _F32),
        compiler_params=pltpu.CompilerParams(
            dimension_semantics=("arbitrary",), vmem_limit_bytes=VMEM_LIMIT_BYTES),
        name="ffn_proj" if proj is not None else "ffn",
    )(*args)


def _inproj_kernel(x_ref, g_ref, w_ref, o_ref):
    xn = _mx(_rms_norm(x_ref[...], g_ref[...]))
    o_ref[...] = jnp.dot(xn, w_ref[...], preferred_element_type=_F32)


def _inproj_call(x2, g, w):
    n, d = x2.shape
    tm = ROW_TILE
    return pl.pallas_call(
        _inproj_kernel,
        grid=(n // tm,),
        in_specs=[pl.BlockSpec((tm, d), lambda i: (i, 0)), _const_spec(g.shape), _const_spec(w.shape)],
        out_specs=pl.BlockSpec((tm, w.shape[1]), lambda i: (i, 0)),
        out_shape=jax.ShapeDtypeStruct((n, w.shape[1]), _F32),
        compiler_params=pltpu.CompilerParams(
            dimension_semantics=("arbitrary",), vmem_limit_bytes=VMEM_LIMIT_BYTES),
        name="inproj",
    )(x2, g, w)


def _pad_rows(w, rows):
    return jnp.pad(w, ((0, rows - w.shape[0]), (0, 0)))


def kernel(x, norm_ffn1, ffn1_w_in, ffn1_w_out, norm_mix, w_in, w_out, pool_w, pool_b, pool_scale,
           hgrn_lb_logits, hgrn_norm, rwkv_mu, rwkv_w0, rwkv_w2, rwkv_a0, rwkv_a2, rwkv_g2, rwkv_k_k,
           rwkv_k_a, rwkv_r_k, rwkv_ln_w, rwkv_ln_b, gla_w2, gla_b, gla_norm, norm_ffn2, ffn2_w_in,
           ffn2_w_out, norm_final):
    b, t, d = x.shape
    depth = norm_ffn1.shape[0]
    gw = GROUP_W
    assert d == D_MODEL and t % TIME_TILE == 0 and (b * t) % ROW_TILE == 0
    assert TIME_TILE % (CHUNK * CHUNK_UNROLL) == 0
    assert sum(FF_BLOCKS) == D_FF and all(w % MXU_TILE == 0 for w in FF_BLOCKS)

    sel = jnp.asarray(_selection_matrix(), _MXU_DTYPE)
    tri = jnp.asarray(np.tril(np.ones((CHUNK, CHUNK), np.float32)), _MXU_DTYPE)
    head = np.arange(gw) // HEAD_DIM
    seg = jnp.asarray((head[:, None] == head[None, :]).astype(np.float32) / HEAD_DIM, _MXU_DTYPE)
    bdmask = jnp.asarray((head[:, None] == head[None, :]).astype(np.float32))
    gf = norm_final.reshape(1, d)

    h = x.reshape(b * t, d)
    for l in range(depth):
        h = _ffn_call(h, norm_ffn1[l].reshape(1, d), _mx(ffn1_w_in[l]), _mx(ffn1_w_out[l]), gf, False)

        w_in_p = _mx(jnp.pad(w_in[l], ((0, 0), (0, D_IN_PAD - D_IN))))
        p = _inproj_call(h, norm_mix[l].reshape(1, d), w_in_p)
        poolw = _mx(jnp.tile(pool_w[l].reshape(gw, POOL_CH), (1, len(POOL_WINDOWS))) * bdmask)
        w2p = _mx(_pad_rows(rwkv_w2[l], LANES))
        a2p = _mx(jnp.pad(rwkv_a2[l], ((RWKV_DECAY_LORA, 0), (0, 0))))
        vecs = jnp.stack([pool_b[l], pool_scale[l], hgrn_norm[l], rwkv_w0[l], rwkv_a0[l], rwkv_k_k[l],
                          rwkv_k_a[l], rwkv_r_k[l], rwkv_ln_w[l], rwkv_ln_b[l], gla_b[l], gla_norm[l]])
        consts = (sel, tri, seg, poolw, w2p, a2p, _mx(rwkv_g2[l]), _mx(_pad_rows(gla_w2[l], LANES)),
                  rwkv_mu[l].reshape(1, 4 * gw), _pad_rows(vecs, _N_VEC), hgrn_lb_logits)
        y = _mixer_call(p.reshape(b, t, D_IN_PAD), consts, l)
        h = _ffn_call(h, norm_ffn2[l].reshape(1, d), _mx(ffn2_w_in[l]), _mx(ffn2_w_out[l]), gf,
                      l == depth - 1, proj=(y.reshape(b * t, d), _mx(w_out[l])))
    return h.reshape(b, t, d)
```

```python
import functools

import numpy as np
import jax
import jax.numpy as jnp
from jax import lax
from jax.experimental import pallas as pl
from jax.experimental.pallas import tpu as pltpu

D_MODEL = 1024
GROUP_W = 256
HEAD_DIM = 64
N_HEADS = GROUP_W // HEAD_DIM
HEAD_SHIFT = HEAD_DIM.bit_length() - 1
QK_SCALE = HEAD_DIM ** -0.5
POOL_WINDOWS = (2, 4, 8, 16)
POOL_CH = GROUP_W // len(POOL_WINDOWS)
MAX_WINDOW = 16
RWKV_DECAY_LORA = 64
RWKV_A_LORA = 64
RWKV_GATE_LORA = 128
RWKV_GN_EPS = 64e-5
GLA_GATE_LORA = 16
GLA_GATE_TAU = 16.0
D_IN = 13 * GROUP_W + GLA_GATE_LORA
D_FF = 2816
NORM_EPS = 1e-6
GATE_FLOOR = 1e-30

LANES = 128
SUBLANES = 8
VMEM_LIMIT_BYTES = 56 * 1024 * 1024

D_IN_PAD = 13 * GROUP_W + LANES
CHUNK = 64
LEVELS = CHUNK.bit_length() - 1
TIME_TILE = 512
CHUNK_UNROLL = 2
SUB_ROWS = CHUNK * CHUNK_UNROLL
ROW_TILE = 512
MXU_TILE = 256
FF_BLOCKS = (1536, 1280)

_P_POOL = 0
_P_HG = GROUP_W
_P_RW = 5 * GROUP_W
_P_GLA = 9 * GROUP_W

_F_HQ, _F_HK, _F_HV, _F_HGD = 0, 1, 2, 3
_F_RR, _F_RK, _F_RV, _F_RA, _F_RB, _F_RW = 4, 5, 6, 7, 8, 9
_F_GQ, _F_GK, _F_GV, _F_GGD = 10, 11, 12, 13
_N_FEAT = 14

(_V_POOL_B, _V_POOL_SCALE, _V_HGRN_NORM, _V_W0, _V_A0, _V_KK, _V_KA, _V_RK, _V_LNW, _V_LNB,
 _V_GLA_B, _V_GLA_NORM) = range(12)
_N_VEC = 16

_MXU_DTYPE = jnp.bfloat16
_F32 = jnp.float32


def _mx(a):
    return a.astype(_MXU_DTYPE)


def _dot(a, b):
    return jnp.dot(_mx(a), _mx(b), preferred_element_type=_F32)


def _dot_nt(a, b):
    return lax.dot_general(_mx(a), _mx(b), (((1,), (1,)), ((), ())), preferred_element_type=_F32)


def _dot_tn(a, b):
    return lax.dot_general(_mx(a), _mx(b), (((0,), (0,)), ((), ())), preferred_element_type=_F32)


def _split(x, parts):
    out = []
    r = x
    for _ in range(parts - 1):
        h = r.astype(_MXU_DTYPE)
        out.append(h)
        r = r - h.astype(_F32)
    out.append(r.astype(_MXU_DTYPE))
    return out


def _dot_exact_lhs(m, x, parts=3):
    m = _mx(m)
    acc = None
    for piece in _split(x, parts):
        t = jnp.dot(m, piece, preferred_element_type=_F32)
        acc = t if acc is None else acc + t
    return acc


def _rms_norm(x, g, eps=NORM_EPS):
    return x * lax.rsqrt(jnp.mean(x * x, axis=-1, keepdims=True) + eps) * g


def _sigmoid(x):
    return 0.5 * jnp.tanh(0.5 * x) + 0.5


def _softplus(x):
    return jnp.maximum(x, 0.0) + jnp.log(1.0 + jnp.exp(-jnp.abs(x)))


def _head_masks():
    lane = lax.broadcasted_iota(jnp.int32, (1, GROUP_W), 1)
    return [lax.shift_right_logical(lane, HEAD_SHIFT) == h for h in range(N_HEADS)]


def _stack_heads(x, hms):
    x = _mx(x)
    zero = jnp.zeros((), x.dtype)
    return jnp.concatenate([jnp.where(hm, x, zero) for hm in hms], axis=0)


def _mid_rows(b, n):
    c, gw = b.shape
    if n == c:
        return jnp.broadcast_to(b[n // 2 - 1:n // 2, :], b.shape)
    b3 = b.reshape(c // n, n, gw)
    return jnp.broadcast_to(b3[:, n // 2 - 1:n // 2, :], b3.shape).reshape(c, gw)


def _gla_chunk(q, k, v, g, sel, hms, xr, bd):
    c = CHUNK
    m = _dot_exact_lhs(sel, g)
    yield
    e2 = m[0:c]
    b = m[c:2 * c]
    f_b = jnp.exp(b)
    kv = jnp.where(bd, _dot_tn(v, k * jnp.exp(b[c - 1:c, :] - b)), 0.0)
    yield
    t_idx = lax.broadcasted_iota(jnp.int32, (c, GROUP_W), 0)
    n_piece = c // SUBLANES

    def pieces_of(x, rows_per_head):
        return [[x[h * rows_per_head + i * SUBLANES:h * rows_per_head + (i + 1) * SUBLANES]
                 for i in range(rows_per_head // SUBLANES)] for h in range(N_HEADS)]

    xr_p = pieces_of(xr, c)
    s_0 = pieces_of(_dot_nt(_stack_heads(q, hms), k), c)
    sc = [[jnp.where(xr_p[h][i] < 1, s_0[h][i], 0.0) for i in range(n_piece)] for h in range(N_HEADS)]
    yield
    for j in range(1, LEVELS + 1):
        n = 1 << j
        upper = (t_idx & (n - 1)) >= n // 2
        if j == 1:
            ex = jnp.where(upper, g, 0.0)
        elif j == 2:
            ex = e2
        else:
            bm = _mid_rows(b, n)
            ex = jnp.where(upper, b - bm, bm - b)
        f_j = jnp.exp(ex)
        qt = jnp.where(upper, q * f_j, 0.0)
        kt = jnp.where(upper, 0.0, k * f_j)
        if n // 2 >= SUBLANES:
            up_idx = [i for i in range(n_piece) if (i * SUBLANES) % n >= n // 2]
            qt = jnp.concatenate([qt[i * SUBLANES:(i + 1) * SUBLANES] for i in up_idx], axis=0)
        else:
            up_idx = list(range(n_piece))
        s_j = pieces_of(_dot_nt(_stack_heads(qt, hms), kt), len(up_idx) * SUBLANES)
        yield
        for h in range(N_HEADS):
            for pos, i in enumerate(up_idx):
                add = s_j[h][pos] if j == LEVELS else jnp.where(xr_p[h][i] < n, s_j[h][pos], 0.0)
                sc[h][i] = sc[h][i] + add
    v_m = _mx(v)
    zero = jnp.zeros((), v_m.dtype)
    o = None
    for h in range(N_HEADS):
        part = _dot(jnp.concatenate(sc[h], axis=0), jnp.where(hms[h], v_m, zero))
        o = part if o is None else o + part
    yield
    yield o, q * f_b, f_b[c - 1:c, :], kv


def _gla_state_step(pre, st):
    o_intra, qb, e_last, kv = pre
    return o_intra + _dot_nt(qb, st), st * e_last + kv


def _rwkv_chunk(r, k, v, al, be, lw, tri, hms, bd_c, bd_g, eye_g):
    c = CHUNK
    hc = N_HEADS * c
    gw = GROUP_W

    def bdiag(x_row):
        return jnp.where(bd_c, jnp.concatenate([_mx(x_row)] * N_HEADS, axis=0), jnp.zeros((), _MXU_DTYPE))

    cw = _dot_exact_lhs(tri, lw)
    yield
    cw_last = cw[c - 1:c, :]
    e_neg = jnp.exp(-cw)
    e_rem = jnp.exp(cw_last - cw)
    a_t = al * jnp.exp(cw - lw)
    r_t = r * jnp.exp(cw)
    prod = _dot_nt(jnp.concatenate([a_t, r_t], axis=0),
                   jnp.concatenate([_stack_heads(be * e_neg, hms), _stack_heads(k * e_neg, hms)], axis=0))
    yield
    t_r = lax.broadcasted_iota(jnp.int32, (c, hc), 0)
    s_r = lax.broadcasted_iota(jnp.int32, (c, hc), 1) & (c - 1)
    l_row = jnp.where(s_r < t_r, prod[0:c, 0:hc], 0.0)
    ak_row = jnp.where(s_r < t_r, prod[0:c, hc:2 * hc], 0.0)
    rb_row = jnp.where(s_r <= t_r, prod[c:2 * c, 0:hc], 0.0)
    rk_row = jnp.where(s_r <= t_r, prod[c:2 * c, hc:2 * hc], 0.0)

    t_row = jnp.where(s_r == t_r, 1.0, 0.0) - l_row
    p_row = _dot(l_row, bdiag(l_row))
    yield
    for i in range(LEVELS - 1):
        last = i == LEVELS - 2
        res = _dot(t_row if last else jnp.concatenate([p_row, t_row], axis=0), bdiag(p_row))
        yield
        if last:
            t_row = t_row + res
        else:
            p_row = res[0:c]
            t_row = t_row + res[c:2 * c]

    v_st = _stack_heads(v, hms)
    mv = _dot(ak_row, v_st)
    yield
    tm = _dot(t_row, jnp.concatenate([_stack_heads(a_t, hms), _stack_heads(mv, hms)], axis=1))
    yield
    ta = tm[:, 0:gw]
    uv = tm[:, gw:2 * gw]
    qe = r_t - _dot(rb_row, _stack_heads(ta, hms))
    yield
    ol = _dot(jnp.concatenate([rk_row, -rb_row], axis=1),
              jnp.concatenate([v_st, _stack_heads(uv, hms)], axis=0))
    yield
    b_r = be * e_rem
    gm = jnp.where(eye_g, jnp.exp(cw_last), 0.0) - jnp.where(bd_g, _dot_tn(b_r, ta), 0.0)
    yield
    hm = jnp.where(bd_g, _dot_tn(jnp.concatenate([k * e_rem, -b_r], axis=0),
                                 jnp.concatenate([v, uv], axis=0)), 0.0)
    yield
    yield gm, hm, qe, ol


def _rwkv_state_step(pre, p):
    gm, hm, qe, ol = pre
    gw = GROUP_W
    res = _dot(jnp.concatenate([gm, qe], axis=0), p)
    return res[gw:] + ol, res[0:gw] + hm


def _interleave(gens):
    results = [None] * len(gens)
    live = list(range(len(gens)))
    while live:
        for i in list(live):
            out = next(gens[i])
            if out is not None:
                results[i] = out
                live.remove(i)
    return results


def _mixer_kernel(p_ref, sel_ref, tri_ref, seg_ref, poolw_ref, w2_ref, a2_ref, g2_ref, glaw2_ref,
                  mu_ref, vec_ref, lb_ref, y_ref,
                  feat, osc, aux, pool_prev, rw_prev, hg_st, rw_p, gl_st, *, layer):
    tb = TIME_TILE
    gw = GROUP_W
    sub = SUB_ROWS
    n_sub = tb // sub
    c = CHUNK
    hc = N_HEADS * c
    ti = pl.program_id(1)

    @pl.when(ti == 0)
    def _():
        pool_prev[...] = jnp.zeros_like(pool_prev)
        rw_prev[...] = jnp.zeros_like(rw_prev)
        hg_st[...] = jnp.zeros_like(hg_st)
        rw_p[...] = jnp.zeros_like(rw_p)
        gl_st[...] = jnp.zeros_like(gl_st)

    def vec(i):
        return vec_ref[i:i + 1, :]

    seg = seg_ref[...]
    hms = _head_masks()

    def features(r0):
        def pcol(base, i, n=1):
            return p_ref[0, r0:r0 + sub, base + i * gw: base + (i + n) * gw]

        def put(i, val):
            feat[r0:r0 + sub, i * gw:(i + 1) * gw] = val

        p_pool = pcol(_P_POOL, 0)
        before = pool_prev[...] if r0 == 0 else p_ref[0, r0 - MAX_WINDOW:r0, _P_POOL:_P_POOL + gw]
        ext = jnp.concatenate([before, p_pool], axis=0)
        if r0 + sub == tb:
            pool_prev[...] = p_pool[sub - MAX_WINDOW:, :]
        lane = lax.broadcasted_iota(jnp.int32, (sub, gw), 1)
        grp = lax.shift_right_logical(lane, HEAD_SHIFT)
        tpos = lax.broadcasted_iota(jnp.int32, (sub, gw), 0) + (ti * tb + r0)
        win = ext
        win_sum = jnp.zeros((sub, gw), _F32)
        count = jnp.zeros((sub, gw), _F32)
        for gi, w in enumerate(POOL_WINDOWS):
            sh = w // 2
            while sh < w:
                win = win + pltpu.roll(win, sh, 0)
                sh *= 2
            win_sum = jnp.where(grp == gi, win[MAX_WINDOW:, :], win_sum)
            count = jnp.where(grp == gi, jnp.minimum(tpos + 1, w).astype(_F32), count)
        pooled = win_sum / count - p_pool
        y_pool = (_dot(pooled, poolw_ref[...]) + vec(_V_POOL_B)) * vec(_V_POOL_SCALE)
        y_ref[0, r0:r0 + sub, 0:gw] = y_pool.astype(y_ref.dtype)
        yield

        sm = jnp.exp(lb_ref[...] - jnp.max(lb_ref[...], axis=0, keepdims=True))
        sm = sm / jnp.sum(sm, axis=0, keepdims=True)
        lb = jnp.sum(sm[0:layer + 1], axis=0, keepdims=True) - sm[0:1]
        hq = pcol(_P_HG, 0)
        f = lb + (1.0 - lb) * _sigmoid(pcol(_P_HG, 1))
        put(_F_HQ, hq * _sigmoid(hq) * QK_SCALE)
        put(_F_HK, 1.0 - f)
        put(_F_HV, pcol(_P_HG, 2))
        put(_F_HGD, jnp.log(jnp.maximum(f, GATE_FLOOR)))
        yield

        p_rw = pcol(_P_RW, 0, 4)
        before = rw_prev[...] if r0 == 0 else p_ref[0, r0 - 1:r0, _P_RW:_P_RW + 4 * gw]
        row0 = lax.broadcasted_iota(jnp.int32, p_rw.shape, 0) == 0
        prev = jnp.where(row0, before, pltpu.roll(p_rw, 1, 0))
        if r0 + sub == tb:
            rw_prev[...] = p_rw[sub - 1:sub, :]
        p_rw = p_rw + (prev - p_rw) * mu_ref[...]
        rr = p_rw[:, 0:gw]
        rk = p_rw[:, gw:2 * gw]
        rv = p_rw[:, 2 * gw:3 * gw]
        xwa = p_rw[:, 3 * gw:3 * gw + LANES]
        xg = p_rw[:, 3 * gw + LANES:4 * gw]
        w_log = -_softplus(-(vec(_V_W0) + _dot(jnp.tanh(xwa), w2_ref[...]))) - 0.5
        yield
        a = _sigmoid(vec(_V_A0) + _dot(xwa, a2_ref[...]))
        yield
        aux[r0:r0 + sub, gw:2 * gw] = _dot(_sigmoid(xg), g2_ref[...])
        yield
        kk = rk * vec(_V_KK)
        kk = kk / jnp.maximum(jnp.sqrt(_dot(kk * kk, seg) * HEAD_DIM), 1e-12)
        yield
        rk = rk * (1.0 + (a - 1.0) * vec(_V_KA))
        put(_F_RR, rr)
        put(_F_RK, rk)
        put(_F_RV, rv)
        put(_F_RA, kk)
        put(_F_RB, kk * a)
        put(_F_RW, -jnp.exp(w_log))
        aux[r0:r0 + sub, 0:gw] = _dot(rr * rk * vec(_V_RK), seg) * HEAD_DIM * rv
        yield

        ga = p_ref[0, r0:r0 + sub, _P_GLA + 4 * gw:_P_GLA + 4 * gw + LANES]
        z = _dot(ga, glaw2_ref[...]) + vec(_V_GLA_B)
        put(_F_GQ, pcol(_P_GLA, 0) * QK_SCALE)
        put(_F_GK, pcol(_P_GLA, 1))
        put(_F_GV, pcol(_P_GLA, 2))
        put(_F_GGD, -_softplus(-z) / GLA_GATE_TAU)
        yield
        yield True

    def outputs(r0):
        def pcol(base, i):
            return p_ref[0, r0:r0 + sub, base + i * gw: base + (i + 1) * gw]

        o_h = osc[r0:r0 + sub, 0:gw]
        y_hg = _rms_norm(o_h, vec(_V_HGRN_NORM)) * _sigmoid(pcol(_P_HG, 3))
        y_ref[0, r0:r0 + sub, gw:2 * gw] = y_hg.astype(y_ref.dtype)
        yield
        o_r = osc[r0:r0 + sub, gw:2 * gw]
        cen = o_r - _dot(o_r, seg)
        yield
        var = _dot(cen * cen, seg)
        gn = cen * lax.rsqrt(var + RWKV_GN_EPS) * vec(_V_LNW) + vec(_V_LNB)
        y_rw = (gn + aux[r0:r0 + sub, 0:gw]) * aux[r0:r0 + sub, gw:2 * gw]
        y_ref[0, r0:r0 + sub, 2 * gw:3 * gw] = y_rw.astype(y_ref.dtype)
        yield
        o_g = osc[r0:r0 + sub, 2 * gw:3 * gw]
        ms = _dot(o_g * o_g, seg)
        gg = pcol(_P_GLA, 3)
        y_gl = o_g * lax.rsqrt(ms + NORM_EPS) * vec(_V_GLA_NORM) * (gg * _sigmoid(gg))
        y_ref[0, r0:r0 + sub, 3 * gw:4 * gw] = y_gl.astype(y_ref.dtype)
        yield
        yield True

    xr = (lax.broadcasted_iota(jnp.int32, (hc, c), 0) & (c - 1)) ^ lax.broadcasted_iota(jnp.int32, (hc, c), 1)
    bd_c = (lax.shift_right_logical(lax.broadcasted_iota(jnp.int32, (hc, hc), 0), LEVELS)
            == lax.shift_right_logical(lax.broadcasted_iota(jnp.int32, (hc, hc), 1), LEVELS))
    row_g = lax.broadcasted_iota(jnp.int32, (gw, gw), 0)
    col_g = lax.broadcasted_iota(jnp.int32, (gw, gw), 1)
    bd_g = lax.shift_right_logical(row_g, HEAD_SHIFT) == lax.shift_right_logical(col_g, HEAD_SHIFT)
    eye_g = row_g == col_g
    sel = sel_ref[...]
    tri = tri_ref[...]

    _interleave([features(0)])
    p_rw_st, st_h, st_g = rw_p[...], hg_st[...], gl_st[...]
    for s_i in range(n_sub):
        gens = []
        for u in range(CHUNK_UNROLL):
            r0 = s_i * sub + u * c

            def ft(i, r0=r0):
                return feat[r0:r0 + c, i * gw:(i + 1) * gw]

            gens += [
                _rwkv_chunk(ft(_F_RR), ft(_F_RK), ft(_F_RV), ft(_F_RA), ft(_F_RB), ft(_F_RW),
                            tri, hms, bd_c, bd_g, eye_g),
                _gla_chunk(ft(_F_HQ), ft(_F_HK), ft(_F_HV), ft(_F_HGD), sel, hms, xr, bd_g),
                _gla_chunk(ft(_F_GQ), ft(_F_GK), ft(_F_GV), ft(_F_GGD), sel, hms, xr, bd_g)]
        if s_i + 1 < n_sub:
            gens.append(features((s_i + 1) * sub))
        if s_i >= 1:
            gens.append(outputs((s_i - 1) * sub))
        pre = _interleave(gens)
        for u in range(CHUNK_UNROLL):
            r0 = s_i * sub + u * c
            o_r, p_rw_st = _rwkv_state_step(pre[3 * u], p_rw_st)
            o_h, st_h = _gla_state_step(pre[3 * u + 1], st_h)
            o_g, st_g = _gla_state_step(pre[3 * u + 2], st_g)
            osc[r0:r0 + c, 0:gw] = o_h
            osc[r0:r0 + c, gw:2 * gw] = o_r
            osc[r0:r0 + c, 2 * gw:3 * gw] = o_g
    rw_p[...] = p_rw_st
    hg_st[...] = st_h
    gl_st[...] = st_g
    _interleave([outputs((n_sub - 1) * sub)])


def _selection_matrix():
    c = CHUNK
    m = np.zeros((2 * c, c), np.float32)
    for t in range(c):
        mid = (t // 4) * 4 + 1
        if t % 4 >= 2:
            m[t, mid + 1:t + 1] = 1.0
        else:
            m[t, t + 1:mid + 1] = 1.0
        m[c + t, :t + 1] = 1.0
    return m


def _const_spec(shape):
    nd = len(shape)
    return pl.BlockSpec(shape, lambda *_: (0,) * nd, pipeline_mode=pl.Buffered(1))


def _mixer_call(p, consts, layer):
    b, t, _ = p.shape
    tb = TIME_TILE
    gw = GROUP_W
    in_specs = [pl.BlockSpec((1, tb, D_IN_PAD), lambda bi, ti: (bi, ti, 0))]
    in_specs += [_const_spec(a.shape) for a in consts]
    return pl.pallas_call(
        functools.partial(_mixer_kernel, layer=layer),
        grid=(b, t // tb),
        in_specs=in_specs,
        out_specs=pl.BlockSpec((1, tb, D_MODEL), lambda bi, ti: (bi, ti, 0)),
        out_shape=jax.ShapeDtypeStruct((b, t, D_MODEL), _MXU_DTYPE),
        scratch_shapes=[
            pltpu.VMEM((tb, _N_FEAT * gw), _F32),
            pltpu.VMEM((tb, 3 * gw), _F32),
            pltpu.VMEM((tb, 2 * gw), _F32),
            pltpu.VMEM((MAX_WINDOW, gw), _F32),
            pltpu.VMEM((1, 4 * gw), _F32),
            pltpu.VMEM((gw, gw), _F32),
            pltpu.VMEM((gw, gw), _F32),
            pltpu.VMEM((gw, gw), _F32),
        ],
        compiler_params=pltpu.CompilerParams(
            dimension_semantics=("arbitrary", "arbitrary"), vmem_limit_bytes=VMEM_LIMIT_BYTES),
        name=f"mixer_l{layer}",
    )(p, *consts)


def _ffn_kernel(*refs, final, fused_proj):
    if fused_proj:
        x_ref, y_ref, wp_ref, g_ref, win_ref, wout_ref, gf_ref, o_ref = refs
        x = x_ref[...] + jnp.dot(y_ref[...], wp_ref[...], preferred_element_type=_F32)
    else:
        x_ref, g_ref, win_ref, wout_ref, gf_ref, o_ref = refs
        x = x_ref[...]
    xn = _mx(_rms_norm(x, g_ref[...]))
    acc = jnp.zeros(x.shape, _F32)
    lo = 0
    for width in FF_BLOCKS:
        gate = jnp.dot(xn, win_ref[:, lo:lo + width], preferred_element_type=_F32)
        up = jnp.dot(xn, win_ref[:, D_FF + lo:D_FF + lo + width], preferred_element_type=_F32)
        act = _mx(gate * _sigmoid(gate) * up)
        acc = acc + jnp.dot(act, wout_ref[lo:lo + width, :], preferred_element_type=_F32)
        lo += width
    y = x + 0.5 * acc
    if final:
        y = _rms_norm(y, gf_ref[...])
    o_ref[...] = y


def _ffn_call(x2, g, w_in, w_out, g_final, final, proj=None):
    n, d = x2.shape
    tm = ROW_TILE
    row_spec = pl.BlockSpec((tm, d), lambda i: (i, 0))
    args = [x2]
    in_specs = [row_spec]
    if proj is not None:
        y2, w_proj = proj
        args += [y2, w_proj]
        in_specs += [pl.BlockSpec((tm, y2.shape[1]), lambda i: (i, 0)), _const_spec(w_proj.shape)]
    args += [g, w_in, w_out, g_final]
    in_specs += [_const_spec(a.shape) for a in (g, w_in, w_out, g_final)]
    return pl.pallas_call(
        functools.partial(_ffn_kernel, final=final, fused_proj=proj is not None),
        grid=(n // tm,),
        in_specs=in_specs,
        out_specs=row_spec,
        out_shape=jax.ShapeDtypeStruct((n, d), _F32),
        compiler_params=pltpu.CompilerParams(
            dimension_semantics=("arbitrary",), vmem_limit_bytes=VMEM_LIMIT_BYTES),
        name="ffn_proj" if proj is not None else "ffn",
    )(*args)


def _inproj_kernel(x_ref, g_ref, w_ref, o_ref):
    xn = _mx(_rms_norm(x_ref[...], g_ref[...]))
    o_ref[...] = jnp.dot(xn, w_ref[...], preferred_element_type=_F32)


def _inproj_call(x2, g, w):
    n, d = x2.shape
    tm = ROW_TILE
    return pl.pallas_call(
        _inproj_kernel,
        grid=(n // tm,),
        in_specs=[pl.BlockSpec((tm, d), lambda i: (i, 0)), _const_spec(g.shape), _const_spec(w.shape)],
        out_specs=pl.BlockSpec((tm, w.shape[1]), lambda i: (i, 0)),
        out_shape=jax.ShapeDtypeStruct((n, w.shape[1]), _F32),
        compiler_params=pltpu.CompilerParams(
            dimension_semantics=("arbitrary",), vmem_limit_bytes=VMEM_LIMIT_BYTES),
        name="inproj",
    )(x2, g, w)


def _pad_rows(w, rows):
    return jnp.pad(w, ((0, rows - w.shape[0]), (0, 0)))


def kernel(x, norm_ffn1, ffn1_w_in, ffn1_w_out, norm_mix, w_in, w_out, pool_w, pool_b, pool_scale,
           hgrn_lb_logits, hgrn_norm, rwkv_mu, rwkv_w0, rwkv_w2, rwkv_a0, rwkv_a2, rwkv_g2, rwkv_k_k,
           rwkv_k_a, rwkv_r_k, rwkv_ln_w, rwkv_ln_b, gla_w2, gla_b, gla_norm, norm_ffn2, ffn2_w_in,
           ffn2_w_out, norm_final):
    b, t, d = x.shape
    depth = norm_ffn1.shape[0]
    gw = GROUP_W
    assert d == D_MODEL and t % TIME_TILE == 0 and (b * t) % ROW_TILE == 0
    assert TIME_TILE % (CHUNK * CHUNK_UNROLL) == 0
    assert sum(FF_BLOCKS) == D_FF and all(w % MXU_TILE == 0 for w in FF_BLOCKS)

    sel = jnp.asarray(_selection_matrix(), _MXU_DTYPE)
    tri = jnp.asarray(np.tril(np.ones((CHUNK, CHUNK), np.float32)), _MXU_DTYPE)
    head = np.arange(gw) // HEAD_DIM
    seg = jnp.asarray((head[:, None] == head[None, :]).astype(np.float32) / HEAD_DIM, _MXU_DTYPE)
    bdmask = jnp.asarray((head[:, None] == head[None, :]).astype(np.float32))
    gf = norm_final.reshape(1, d)

    h = x.reshape(b * t, d)
    for l in range(depth):
        h = _ffn_call(h, norm_ffn1[l].reshape(1, d), _mx(ffn1_w_in[l]), _mx(ffn1_w_out[l]), gf, False)

        w_in_p = _mx(jnp.pad(w_in[l], ((0, 0), (0, D_IN_PAD - D_IN))))
        p = _inproj_call(h, norm_mix[l].reshape(1, d), w_in_p)
        poolw = _mx(jnp.tile(pool_w[l].reshape(gw, POOL_CH), (1, len(POOL_WINDOWS))) * bdmask)
        w2p = _mx(_pad_rows(rwkv_w2[l], LANES))
        a2p = _mx(jnp.pad(rwkv_a2[l], ((RWKV_DECAY_LORA, 0), (0, 0))))
        vecs = jnp.stack([pool_b[l], pool_scale[l], hgrn_norm[l], rwkv_w0[l], rwkv_a0[l], rwkv_k_k[l],
                          rwkv_k_a[l], rwkv_r_k[l], rwkv_ln_w[l], rwkv_ln_b[l], gla_b[l], gla_norm[l]])
        consts = (sel, tri, seg, poolw, w2p, a2p, _mx(rwkv_g2[l]), _mx(_pad_rows(gla_w2[l], LANES)),
                  rwkv_mu[l].reshape(1, 4 * gw), _pad_rows(vecs, _N_VEC), hgrn_lb_logits)
        y = _mixer_call(p.reshape(b, t, D_IN_PAD), consts, l)
        h = _ffn_call(h, norm_ffn2[l].reshape(1, d), _mx(ffn2_w_in[l]), _mx(ffn2_w_out[l]), gf,
                      l == depth - 1, proj=(y.reshape(b * t, d), _mx(w_out[l])))
    return h.reshape(b, t, d)
```

```python
import functools

import numpy as np
import jax
import jax.numpy as jnp
from jax import lax
from jax.experimental import pallas as pl
from jax.experimental.pallas import tpu as pltpu

D_MODEL = 1024
GROUP_W = 256
HEAD_DIM = 64
N_HEADS = GROUP_W // HEAD_DIM
HEAD_SHIFT = HEAD_DIM.bit_length() - 1
QK_SCALE = HEAD_DIM ** -0.5
POOL_WINDOWS = (2, 4, 8, 16)
POOL_CH = GROUP_W // len(POOL_WINDOWS)
MAX_WINDOW = 16
RWKV_DECAY_LORA = 64
RWKV_A_LORA = 64
RWKV_GATE_LORA = 128
RWKV_GN_EPS = 64e-5
GLA_GATE_LORA = 16
GLA_GATE_TAU = 16.0
D_IN = 13 * GROUP_W + GLA_GATE_LORA
D_FF = 2816
NORM_EPS = 1e-6
GATE_FLOOR = 1e-30

LANES = 128
SUBLANES = 8
VMEM_LIMIT_BYTES = 56 * 1024 * 1024

D_IN_PAD = 13 * GROUP_W + LANES
CHUNK = 64
LEVELS = CHUNK.bit_length() - 1
TIME_TILE = 512
CHUNK_UNROLL = 2
SUB_ROWS = CHUNK * CHUNK_UNROLL
ROW_TILE = 512
MXU_TILE = 256
FF_BLOCKS = (1536, 1280)

_P_POOL = 0
_P_HG = GROUP_W
_P_RW = 5 * GROUP_W
_P_GLA = 9 * GROUP_W

_F_HQ, _F_HK, _F_HV, _F_HGD = 0, 1, 2, 3
_F_RR, _F_RK, _F_RV, _F_RA, _F_RB, _F_RW = 4, 5, 6, 7, 8, 9
_F_GQ, _F_GK, _F_GV, _F_GGD = 10, 11, 12, 13
_N_FEAT = 14

(_V_POOL_B, _V_POOL_SCALE, _V_HGRN_NORM, _V_W0, _V_A0, _V_KK, _V_KA, _V_RK, _V_LNW, _V_LNB,
 _V_GLA_B, _V_GLA_NORM) = range(12)
_N_VEC = 16

_MXU_DTYPE = jnp.bfloat16
_F32 = jnp.float32


def _mx(a):
    return a.astype(_MXU_DTYPE)


def _dot(a, b):
    return jnp.dot(_mx(a), _mx(b), preferred_element_type=_F32)


def _dot_nt(a, b):
    return lax.dot_general(_mx(a), _mx(b), (((1,), (1,)), ((), ())), preferred_element_type=_F32)


def _dot_tn(a, b):
    return lax.dot_general(_mx(a), _mx(b), (((0,), (0,)), ((), ())), preferred_element_type=_F32)


def _split(x, parts):
    out = []
    r = x
    for _ in range(parts - 1):
        h = r.astype(_MXU_DTYPE)
        out.append(h)
        r = r - h.astype(_F32)
    out.append(r.astype(_MXU_DTYPE))
    return out


def _dot_exact_lhs(m, x, parts=3):
    m = _mx(m)
    acc = None
    for piece in _split(x, parts):
        t = jnp.dot(m, piece, preferred_element_type=_F32)
        acc = t if acc is None else acc + t
    return acc


def _rms_norm(x, g, eps=NORM_EPS):
    return x * lax.rsqrt(jnp.mean(x * x, axis=-1, keepdims=True) + eps) * g


def _sigmoid(x):
    return 0.5 * jnp.tanh(0.5 * x) + 0.5


def _softplus(x):
    return jnp.maximum(x, 0.0) + jnp.log(1.0 + jnp.exp(-jnp.abs(x)))


def _head_masks():
    lane = lax.broadcasted_iota(jnp.int32, (1, GROUP_W), 1)
    return [lax.shift_right_logical(lane, HEAD_SHIFT) == h for h in range(N_HEADS)]


def _stack_heads(x, hms):
    x = _mx(x)
    zero = jnp.zeros((), x.dtype)
    return jnp.concatenate([jnp.where(hm, x, zero) for hm in hms], axis=0)


def _mid_rows(b, n):
    c, gw = b.shape
    if n == c:
        return jnp.broadcast_to(b[n // 2 - 1:n // 2, :], b.shape)
    b3 = b.reshape(c // n, n, gw)
    return jnp.broadcast_to(b3[:, n // 2 - 1:n // 2, :], b3.shape).reshape(c, gw)


def _gla_chunk(q, k, v, g, sel, hms, xr, bd):
    c = CHUNK
    m = _dot_exact_lhs(sel, g)
    yield
    e2 = m[0:c]
    b = m[c:2 * c]
    f_b = jnp.exp(b)
    kv = jnp.where(bd, _dot_tn(v, k * jnp.exp(b[c - 1:c, :] - b)), 0.0)
    yield
    t_idx = lax.broadcasted_iota(jnp.int32, (c, GROUP_W), 0)
    n_piece = c // SUBLANES

    def pieces_of(x, rows_per_head):
        return [[x[h * rows_per_head + i * SUBLANES:h * rows_per_head + (i + 1) * SUBLANES]
                 for i in range(rows_per_head // SUBLANES)] for h in range(N_HEADS)]

    xr_p = [xr[i * SUBLANES:(i + 1) * SUBLANES] for i in range(n_piece)]
    s_0 = pieces_of(_dot_nt(_stack_heads(q, hms), k), c)
    diag = [xr_i < 1 for xr_i in xr_p]
    sc = [[jnp.where(diag[i], s_0[h][i], 0.0) for i in range(n_piece)] for h in range(N_HEADS)]
    yield
    for j in range(1, LEVELS + 1):
        n = 1 << j
        upper = (t_idx & (n - 1)) >= n // 2
        if j == 1:
            ex = jnp.where(upper, g, 0.0)
        elif j == 2:
            ex = e2
        else:
            bm = _mid_rows(b, n)
            ex = jnp.where(upper, b - bm, bm - b)
        f_j = jnp.exp(ex)
        qt = jnp.where(upper, q * f_j, 0.0)
        kt = jnp.where(upper, 0.0, k * f_j)
        if n // 2 >= SUBLANES:
            up_idx = [i for i in range(n_piece) if (i * SUBLANES) % n >= n // 2]
            qt = jnp.concatenate([qt[i * SUBLANES:(i + 1) * SUBLANES] for i in up_idx], axis=0)
        else:
            up_idx = list(range(n_piece))
        s_j = pieces_of(_dot_nt(_stack_heads(qt, hms), kt), len(up_idx) * SUBLANES)
        yield
        same_block = {i: xr_p[i] < n for i in up_idx}
        for h in range(N_HEADS):
            for pos, i in enumerate(up_idx):
                add = s_j[h][pos] if j == LEVELS else jnp.where(same_block[i], s_j[h][pos], 0.0)
                sc[h][i] = sc[h][i] + add
    sc_row = jnp.concatenate([jnp.concatenate([sc[h][i] for h in range(N_HEADS)], axis=1)
                              for i in range(n_piece)], axis=0)
    o = _dot(sc_row, _stack_heads(v, hms))
    yield
    yield o, q * f_b, f_b[c - 1:c, :], kv


def _gla_state_step(pre, st):
    o_intra, qb, e_last, kv = pre
    return o_intra + _dot_nt(qb, st), st * e_last + kv


def _rwkv_chunk(r, k, v, al, be, lw, tri, hms, bd_c, bd_g, eye_g):
    c = CHUNK
    hc = N_HEADS * c
    gw = GROUP_W

    def bdiag(x_row):
        return jnp.where(bd_c, jnp.concatenate([_mx(x_row)] * N_HEADS, axis=0), jnp.zeros((), _MXU_DTYPE))

    cw = _dot_exact_lhs(tri, lw)
    yield
    cw_last = cw[c - 1:c, :]
    e_neg = jnp.exp(-cw)
    e_rem = jnp.exp(cw_last - cw)
    a_t = al * jnp.exp(cw - lw)
    r_t = r * jnp.exp(cw)
    prod = _dot_nt(jnp.concatenate([a_t, r_t], axis=0),
                   jnp.concatenate([_stack_heads(be * e_neg, hms), _stack_heads(k * e_neg, hms)], axis=0))
    yield
    t_r = lax.broadcasted_iota(jnp.int32, (c, hc), 0)
    s_r = lax.broadcasted_iota(jnp.int32, (c, hc), 1) & (c - 1)
    l_row = jnp.where(s_r < t_r, prod[0:c, 0:hc], 0.0)
    ak_row = jnp.where(s_r < t_r, prod[0:c, hc:2 * hc], 0.0)
    rb_row = jnp.where(s_r <= t_r, prod[c:2 * c, 0:hc], 0.0)
    rk_row = jnp.where(s_r <= t_r, prod[c:2 * c, hc:2 * hc], 0.0)

    t_row = jnp.where(s_r == t_r, 1.0, 0.0) - l_row
    p_row = _dot(l_row, bdiag(l_row))
    yield
    for i in range(LEVELS - 1):
        last = i == LEVELS - 2
        res = _dot(t_row if last else jnp.concatenate([p_row, t_row], axis=0), bdiag(p_row))
        yield
        if last:
            t_row = t_row + res
        else:
            p_row = res[0:c]
            t_row = t_row + res[c:2 * c]

    v_st = _stack_heads(v, hms)
    mv = _dot(ak_row, v_st)
    yield
    tm = _dot(t_row, jnp.concatenate([_stack_heads(a_t, hms), _stack_heads(mv, hms)], axis=1))
    yield
    ta = tm[:, 0:gw]
    uv = tm[:, gw:2 * gw]
    qe = r_t - _dot(rb_row, _stack_heads(ta, hms))
    yield
    ol = _dot(jnp.concatenate([rk_row, -rb_row], axis=1),
              jnp.concatenate([v_st, _stack_heads(uv, hms)], axis=0))
    yield
    b_r = be * e_rem
    gm = jnp.where(eye_g, jnp.exp(cw_last), 0.0) - jnp.where(bd_g, _dot_tn(b_r, ta), 0.0)
    yield
    hm = jnp.where(bd_g, _dot_tn(jnp.concatenate([k * e_rem, -b_r], axis=0),
                                 jnp.concatenate([v, uv], axis=0)), 0.0)
    yield
    yield gm, hm, qe, ol


def _rwkv_state_step(pre, p):
    gm, hm, qe, ol = pre
    gw = GROUP_W
    res = _dot(jnp.concatenate([gm, qe], axis=0), p)
    return res[gw:] + ol, res[0:gw] + hm


def _interleave(gens):
    results = [None] * len(gens)
    live = list(range(len(gens)))
    while live:
        for i in list(live):
            out = next(gens[i])
            if out is not None:
                results[i] = out
                live.remove(i)
    return results


def _mixer_kernel(p_ref, sel_ref, tri_ref, seg_ref, poolw_ref, w2_ref, a2_ref, g2_ref, glaw2_ref,
                  mu_ref, vec_ref, lb_ref, y_ref,
                  feat, osc, aux, pool_prev, rw_prev, hg_st, rw_p, gl_st, *, layer):
    tb = TIME_TILE
    gw = GROUP_W
    sub = SUB_ROWS
    n_sub = tb // sub
    c = CHUNK
    hc = N_HEADS * c
    ti = pl.program_id(1)

    @pl.when(ti == 0)
    def _():
        pool_prev[...] = jnp.zeros_like(pool_prev)
        rw_prev[...] = jnp.zeros_like(rw_prev)
        hg_st[...] = jnp.zeros_like(hg_st)
        rw_p[...] = jnp.zeros_like(rw_p)
        gl_st[...] = jnp.zeros_like(gl_st)

    def vec(i):
        return vec_ref[i:i + 1, :]

    seg = seg_ref[...]
    hms = _head_masks()

    def features(r0):
        def pcol(base, i, n=1):
            return p_ref[0, r0:r0 + sub, base + i * gw: base + (i + n) * gw]

        def put(i, val):
            feat[r0:r0 + sub, i * gw:(i + 1) * gw] = val

        p_pool = pcol(_P_POOL, 0)
        before = pool_prev[...] if r0 == 0 else p_ref[0, r0 - MAX_WINDOW:r0, _P_POOL:_P_POOL + gw]
        ext = jnp.concatenate([before, p_pool], axis=0)
        if r0 + sub == tb:
            pool_prev[...] = p_pool[sub - MAX_WINDOW:, :]
        lane = lax.broadcasted_iota(jnp.int32, (sub, gw), 1)
        grp = lax.shift_right_logical(lane, HEAD_SHIFT)
        tpos = lax.broadcasted_iota(jnp.int32, (sub, gw), 0) + (ti * tb + r0)
        win = ext
        win_sum = jnp.zeros((sub, gw), _F32)
        count = jnp.zeros((sub, gw), _F32)
        for gi, w in enumerate(POOL_WINDOWS):
            sh = w // 2
            while sh < w:
                win = win + pltpu.roll(win, sh, 0)
                sh *= 2
            win_sum = jnp.where(grp == gi, win[MAX_WINDOW:, :], win_sum)
            count = jnp.where(grp == gi, jnp.minimum(tpos + 1, w).astype(_F32), count)
        pooled = win_sum / count - p_pool
        y_pool = (_dot(pooled, poolw_ref[...]) + vec(_V_POOL_B)) * vec(_V_POOL_SCALE)
        y_ref[0, r0:r0 + sub, 0:gw] = y_pool.astype(y_ref.dtype)
        yield

        sm = jnp.exp(lb_ref[...] - jnp.max(lb_ref[...], axis=0, keepdims=True))
        sm = sm / jnp.sum(sm, axis=0, keepdims=True)
        lb = jnp.sum(sm[0:layer + 1], axis=0, keepdims=True) - sm[0:1]
        hq = pcol(_P_HG, 0)
        f = lb + (1.0 - lb) * _sigmoid(pcol(_P_HG, 1))
        put(_F_HQ, hq * _sigmoid(hq) * QK_SCALE)
        put(_F_HK, 1.0 - f)
        put(_F_HV, pcol(_P_HG, 2))
        put(_F_HGD, jnp.log(jnp.maximum(f, GATE_FLOOR)))
        yield

        p_rw = pcol(_P_RW, 0, 4)
        before = rw_prev[...] if r0 == 0 else p_ref[0, r0 - 1:r0, _P_RW:_P_RW + 4 * gw]
        row0 = lax.broadcasted_iota(jnp.int32, p_rw.shape, 0) == 0
        prev = jnp.where(row0, before, pltpu.roll(p_rw, 1, 0))
        if r0 + sub == tb:
            rw_prev[...] = p_rw[sub - 1:sub, :]
        p_rw = p_rw + (prev - p_rw) * mu_ref[...]
        rr = p_rw[:, 0:gw]
        rk = p_rw[:, gw:2 * gw]
        rv = p_rw[:, 2 * gw:3 * gw]
        xwa = p_rw[:, 3 * gw:3 * gw + LANES]
        xg = p_rw[:, 3 * gw + LANES:4 * gw]
        w_log = -_softplus(-(vec(_V_W0) + _dot(jnp.tanh(xwa), w2_ref[...]))) - 0.5
        yield
        a = _sigmoid(vec(_V_A0) + _dot(xwa, a2_ref[...]))
        yield
        aux[r0:r0 + sub, gw:2 * gw] = _dot(_sigmoid(xg), g2_ref[...])
        yield
        kk = rk * vec(_V_KK)
        kk = kk / jnp.maximum(jnp.sqrt(_dot(kk * kk, seg) * HEAD_DIM), 1e-12)
        yield
        rk = rk * (1.0 + (a - 1.0) * vec(_V_KA))
        put(_F_RR, rr)
        put(_F_RK, rk)
        put(_F_RV, rv)
        put(_F_RA, kk)
        put(_F_RB, kk * a)
        put(_F_RW, -jnp.exp(w_log))
        aux[r0:r0 + sub, 0:gw] = _dot(rr * rk * vec(_V_RK), seg) * HEAD_DIM * rv
        yield

        ga = p_ref[0, r0:r0 + sub, _P_GLA + 4 * gw:_P_GLA + 4 * gw + LANES]
        z = _dot(ga, glaw2_ref[...]) + vec(_V_GLA_B)
        put(_F_GQ, pcol(_P_GLA, 0) * QK_SCALE)
        put(_F_GK, pcol(_P_GLA, 1))
        put(_F_GV, pcol(_P_GLA, 2))
        put(_F_GGD, -_softplus(-z) / GLA_GATE_TAU)
        yield
        yield True

    def outputs(r0):
        def pcol(base, i):
            return p_ref[0, r0:r0 + sub, base + i * gw: base + (i + 1) * gw]

        o_h = osc[r0:r0 + sub, 0:gw]
        y_hg = _rms_norm(o_h, vec(_V_HGRN_NORM)) * _sigmoid(pcol(_P_HG, 3))
        y_ref[0, r0:r0 + sub, gw:2 * gw] = y_hg.astype(y_ref.dtype)
        yield
        o_r = osc[r0:r0 + sub, gw:2 * gw]
        cen = o_r - _dot(o_r, seg)
        yield
        var = _dot(cen * cen, seg)
        gn = cen * lax.rsqrt(var + RWKV_GN_EPS) * vec(_V_LNW) + vec(_V_LNB)
        y_rw = (gn + aux[r0:r0 + sub, 0:gw]) * aux[r0:r0 + sub, gw:2 * gw]
        y_ref[0, r0:r0 + sub, 2 * gw:3 * gw] = y_rw.astype(y_ref.dtype)
        yield
        o_g = osc[r0:r0 + sub, 2 * gw:3 * gw]
        ms = _dot(o_g * o_g, seg)
        gg = pcol(_P_GLA, 3)
        y_gl = o_g * lax.rsqrt(ms + NORM_EPS) * vec(_V_GLA_NORM) * (gg * _sigmoid(gg))
        y_ref[0, r0:r0 + sub, 3 * gw:4 * gw] = y_gl.astype(y_ref.dtype)
        yield
        yield True

    xr = lax.broadcasted_iota(jnp.int32, (c, c), 0) ^ lax.broadcasted_iota(jnp.int32, (c, c), 1)
    bd_c = (lax.shift_right_logical(lax.broadcasted_iota(jnp.int32, (hc, hc), 0), LEVELS)
            == lax.shift_right_logical(lax.broadcasted_iota(jnp.int32, (hc, hc), 1), LEVELS))
    row_g = lax.broadcasted_iota(jnp.int32, (gw, gw), 0)
    col_g = lax.broadcasted_iota(jnp.int32, (gw, gw), 1)
    bd_g = lax.shift_right_logical(row_g, HEAD_SHIFT) == lax.shift_right_logical(col_g, HEAD_SHIFT)
    eye_g = row_g == col_g
    sel = sel_ref[...]
    tri = tri_ref[...]

    _interleave([features(0)])
    p_rw_st, st_h, st_g = rw_p[...], hg_st[...], gl_st[...]
    for s_i in range(n_sub):
        gens = []
        for u in range(CHUNK_UNROLL):
            r0 = s_i * sub + u * c

            def ft(i, r0=r0):
                return feat[r0:r0 + c, i * gw:(i + 1) * gw]

            gens += [
                _rwkv_chunk(ft(_F_RR), ft(_F_RK), ft(_F_RV), ft(_F_RA), ft(_F_RB), ft(_F_RW),
                            tri, hms, bd_c, bd_g, eye_g),
                _gla_chunk(ft(_F_HQ), ft(_F_HK), ft(_F_HV), ft(_F_HGD), sel, hms, xr, bd_g),
                _gla_chunk(ft(_F_GQ), ft(_F_GK), ft(_F_GV), ft(_F_GGD), sel, hms, xr, bd_g)]
        if s_i + 1 < n_sub:
            gens.append(features((s_i + 1) * sub))
        if s_i >= 1:
            gens.append(outputs((s_i - 1) * sub))
        pre = _interleave(gens)
        for u in range(CHUNK_UNROLL):
            r0 = s_i * sub + u * c
            o_r, p_rw_st = _rwkv_state_step(pre[3 * u], p_rw_st)
            o_h, st_h = _gla_state_step(pre[3 * u + 1], st_h)
            o_g, st_g = _gla_state_step(pre[3 * u + 2], st_g)
            osc[r0:r0 + c, 0:gw] = o_h
            osc[r0:r0 + c, gw:2 * gw] = o_r
            osc[r0:r0 + c, 2 * gw:3 * gw] = o_g
    rw_p[...] = p_rw_st
    hg_st[...] = st_h
    gl_st[...] = st_g
    _interleave([outputs((n_sub - 1) * sub)])


def _selection_matrix():
    c = CHUNK
    m = np.zeros((2 * c, c), np.float32)
    for t in range(c):
        mid = (t // 4) * 4 + 1
        if t % 4 >= 2:
            m[t, mid + 1:t + 1] = 1.0
        else:
            m[t, t + 1:mid + 1] = 1.0
        m[c + t, :t + 1] = 1.0
    return m


def _const_spec(shape):
    nd = len(shape)
    return pl.BlockSpec(shape, lambda *_: (0,) * nd, pipeline_mode=pl.Buffered(1))


def _layer_spec(shape, layer):
    return pl.BlockSpec((None,) + tuple(shape[1:]), lambda *_: (layer, 0, 0), pipeline_mode=pl.Buffered(1))


def _mixer_call(p, consts, layer):
    b, t, _ = p.shape
    tb = TIME_TILE
    gw = GROUP_W
    in_specs = [pl.BlockSpec((1, tb, D_IN_PAD), lambda bi, ti: (bi, ti, 0))]
    in_specs += [_const_spec(a.shape) for a in consts]
    return pl.pallas_call(
        functools.partial(_mixer_kernel, layer=layer),
        grid=(b, t // tb),
        in_specs=in_specs,
        out_specs=pl.BlockSpec((1, tb, D_MODEL), lambda bi, ti: (bi, ti, 0)),
        out_shape=jax.ShapeDtypeStruct((b, t, D_MODEL), _MXU_DTYPE),
        scratch_shapes=[
            pltpu.VMEM((tb, _N_FEAT * gw), _F32),
            pltpu.VMEM((tb, 3 * gw), _F32),
            pltpu.VMEM((tb, 2 * gw), _F32),
            pltpu.VMEM((MAX_WINDOW, gw), _F32),
            pltpu.VMEM((1, 4 * gw), _F32),
            pltpu.VMEM((gw, gw), _F32),
            pltpu.VMEM((gw, gw), _F32),
            pltpu.VMEM((gw, gw), _F32),
        ],
        compiler_params=pltpu.CompilerParams(
            dimension_semantics=("arbitrary", "arbitrary"), vmem_limit_bytes=VMEM_LIMIT_BYTES),
        name=f"mixer_l{layer}",
    )(p, *consts)


def _ffn_kernel(*refs, final, fused_proj):
    if fused_proj:
        x_ref, y_ref, wp_ref, g_ref, win_ref, wout_ref, gf_ref, o_ref = refs
        x = x_ref[...] + jnp.dot(y_ref[...], wp_ref[...], preferred_element_type=_F32)
    else:
        x_ref, g_ref, win_ref, wout_ref, gf_ref, o_ref = refs
        x = x_ref[...]
    xn = _mx(_rms_norm(x, g_ref[...]))
    acc = jnp.zeros(x.shape, _F32)
    lo = 0
    for width in FF_BLOCKS:
        gate = jnp.dot(xn, win_ref[:, lo:lo + width], preferred_element_type=_F32)
        up = jnp.dot(xn, win_ref[:, D_FF + lo:D_FF + lo + width], preferred_element_type=_F32)
        act = _mx(gate * _sigmoid(gate) * up)
        acc = acc + jnp.dot(act, wout_ref[lo:lo + width, :], preferred_element_type=_F32)
        lo += width
    y = x + 0.5 * acc
    if final:
        y = _rms_norm(y, gf_ref[...])
    o_ref[...] = y


def _ffn_call(x2, g, w_in, w_out, g_final, final, layer, proj=None):
    n, d = x2.shape
    tm = ROW_TILE
    row_spec = pl.BlockSpec((tm, d), lambda i: (i, 0))
    args = [x2]
    in_specs = [row_spec]
    if proj is not None:
        y2, w_proj = proj
        args += [y2, w_proj]
        in_specs += [pl.BlockSpec((tm, y2.shape[1]), lambda i: (i, 0)), _layer_spec(w_proj.shape, layer)]
    args += [g, w_in, w_out, g_final]
    in_specs += [_const_spec(g.shape), _layer_spec(w_in.shape, layer), _layer_spec(w_out.shape, layer),
                 _const_spec(g_final.shape)]
    return pl.pallas_call(
        functools.partial(_ffn_kernel, final=final, fused_proj=proj is not None),
        grid=(n // tm,),
        in_specs=in_specs,
        out_specs=row_spec,
        out_shape=jax.ShapeDtypeStruct((n, d), _F32),
        compiler_params=pltpu.CompilerParams(
            dimension_semantics=("arbitrary",), vmem_limit_bytes=VMEM_LIMIT_BYTES),
        name="ffn_proj" if proj is not None else "ffn",
    )(*args)


def _inproj_kernel(x_ref, g_ref, w_ref, o_ref):
    xn = _mx(_rms_norm(x_ref[...], g_ref[...]))
    o_ref[...] = jnp.dot(xn, w_ref[...], preferred_element_type=_F32)


def _inproj_call(x2, g, w, layer):
    n, d = x2.shape
    tm = ROW_TILE
    return pl.pallas_call(
        _inproj_kernel,
        grid=(n // tm,),
        in_specs=[pl.BlockSpec((tm, d), lambda i: (i, 0)), _const_spec(g.shape), _layer_spec(w.shape, layer)],
        out_specs=pl.BlockSpec((tm, w.shape[2]), lambda i: (i, 0)),
        out_shape=jax.ShapeDtypeStruct((n, w.shape[2]), _F32),
        compiler_params=pltpu.CompilerParams(
            dimension_semantics=("arbitrary",), vmem_limit_bytes=VMEM_LIMIT_BYTES),
        name="inproj",
    )(x2, g, w)


def _pad_rows(w, rows):
    return jnp.pad(w, ((0, rows - w.shape[0]), (0, 0)))


def kernel(x, norm_ffn1, ffn1_w_in, ffn1_w_out, norm_mix, w_in, w_out, pool_w, pool_b, pool_scale,
           hgrn_lb_logits, hgrn_norm, rwkv_mu, rwkv_w0, rwkv_w2, rwkv_a0, rwkv_a2, rwkv_g2, rwkv_k_k,
           rwkv_k_a, rwkv_r_k, rwkv_ln_w, rwkv_ln_b, gla_w2, gla_b, gla_norm, norm_ffn2, ffn2_w_in,
           ffn2_w_out, norm_final):
    b, t, d = x.shape
    depth = norm_ffn1.shape[0]
    gw = GROUP_W
    assert d == D_MODEL and t % TIME_TILE == 0 and (b * t) % ROW_TILE == 0
    assert TIME_TILE % (CHUNK * CHUNK_UNROLL) == 0
    assert sum(FF_BLOCKS) == D_FF and all(w % MXU_TILE == 0 for w in FF_BLOCKS)

    sel = jnp.asarray(_selection_matrix(), _MXU_DTYPE)
    tri = jnp.asarray(np.tril(np.ones((CHUNK, CHUNK), np.float32)), _MXU_DTYPE)
    head = np.arange(gw) // HEAD_DIM
    seg = jnp.asarray((head[:, None] == head[None, :]).astype(np.float32) / HEAD_DIM, _MXU_DTYPE)
    bdmask = jnp.asarray((head[:, None] == head[None, :]).astype(np.float32))
    gf = norm_final.reshape(1, d)

    w1_in, w1_out, w2_in, w2_out, w_mix_out = (_mx(w) for w in (ffn1_w_in, ffn1_w_out, ffn2_w_in, ffn2_w_out, w_out))
    w_mix_in = _mx(jnp.pad(w_in, ((0, 0), (0, 0), (0, D_IN_PAD - D_IN))))

    h = x.reshape(b * t, d)
    for l in range(depth):
        h = _ffn_call(h, norm_ffn1[l].reshape(1, d), w1_in, w1_out, gf, False, l)

        p = _inproj_call(h, norm_mix[l].reshape(1, d), w_mix_in, l)
        poolw = _mx(jnp.tile(pool_w[l].reshape(gw, POOL_CH), (1, len(POOL_WINDOWS))) * bdmask)
        w2p = _mx(_pad_rows(rwkv_w2[l], LANES))
        a2p = _mx(jnp.pad(rwkv_a2[l], ((RWKV_DECAY_LORA, 0), (0, 0))))
        vecs = jnp.stack([pool_b[l], pool_scale[l], hgrn_norm[l], rwkv_w0[l], rwkv_a0[l], rwkv_k_k[l],
                          rwkv_k_a[l], rwkv_r_k[l], rwkv_ln_w[l], rwkv_ln_b[l], gla_b[l], gla_norm[l]])
        consts = (sel, tri, seg, poolw, w2p, a2p, _mx(rwkv_g2[l]), _mx(_pad_rows(gla_w2[l], LANES)),
                  rwkv_mu[l].reshape(1, 4 * gw), _pad_rows(vecs, _N_VEC), hgrn_lb_logits)
        y = _mixer_call(p.reshape(b, t, D_IN_PAD), consts, l)
        h = _ffn_call(h, norm_ffn2[l].reshape(1, d), w2_in, w2_out, gf, l == depth - 1, l,
                      proj=(y.reshape(b * t, d), w_mix_out))
    return h.reshape(b, t, d)
```

```python
import functools

import numpy as np
import jax
import jax.numpy as jnp
from jax import lax
from jax.experimental import pallas as pl
from jax.experimental.pallas import tpu as pltpu

D_MODEL = 1024
GROUP_W = 256
HEAD_DIM = 64
N_HEADS = GROUP_W // HEAD_DIM
HEAD_SHIFT = HEAD_DIM.bit_length() - 1
QK_SCALE = HEAD_DIM ** -0.5
POOL_WINDOWS = (2, 4, 8, 16)
POOL_CH = GROUP_W // len(POOL_WINDOWS)
MAX_WINDOW = 16
RWKV_DECAY_LORA = 64
RWKV_A_LORA = 64
RWKV_GATE_LORA = 128
RWKV_GN_EPS = 64e-5
GLA_GATE_LORA = 16
GLA_GATE_TAU = 16.0
D_IN = 13 * GROUP_W + GLA_GATE_LORA
D_FF = 2816
NORM_EPS = 1e-6
GATE_FLOOR = 1e-30

LANES = 128
SUBLANES = 8
VMEM_LIMIT_BYTES = 56 * 1024 * 1024

D_IN_PAD = 13 * GROUP_W + LANES
CHUNK = 64
LEVELS = CHUNK.bit_length() - 1
TIME_TILE = 512
CHUNK_UNROLL = 2
SUB_ROWS = CHUNK * CHUNK_UNROLL
ROW_TILE = 512
MXU_TILE = 256
FF_BLOCKS = (1536, 1280)

_P_POOL = 0
_P_HG = GROUP_W
_P_RW = 5 * GROUP_W
_P_GLA = 9 * GROUP_W

_F_HQ, _F_HK, _F_HV, _F_HGD = 0, 1, 2, 3
_F_RR, _F_RK, _F_RV, _F_RA, _F_RB, _F_RW = 4, 5, 6, 7, 8, 9
_F_GQ, _F_GK, _F_GV, _F_GGD = 10, 11, 12, 13
_N_FEAT = 14

(_V_POOL_B, _V_POOL_SCALE, _V_HGRN_NORM, _V_W0, _V_A0, _V_KK, _V_KA, _V_RK, _V_LNW, _V_LNB,
 _V_GLA_B, _V_GLA_NORM) = range(12)
_N_VEC = 16

_MXU_DTYPE = jnp.bfloat16
_F32 = jnp.float32


def _mx(a):
    return a.astype(_MXU_DTYPE)


def _dot(a, b):
    return jnp.dot(_mx(a), _mx(b), preferred_element_type=_F32)


def _dot_nt(a, b):
    return lax.dot_general(_mx(a), _mx(b), (((1,), (1,)), ((), ())), preferred_element_type=_F32)


def _dot_tn(a, b):
    return lax.dot_general(_mx(a), _mx(b), (((0,), (0,)), ((), ())), preferred_element_type=_F32)


def _split(x, parts):
    out = []
    r = x
    for _ in range(parts - 1):
        h = r.astype(_MXU_DTYPE)
        out.append(h)
        r = r - h.astype(_F32)
    out.append(r.astype(_MXU_DTYPE))
    return out


def _dot_exact_lhs(m, x, parts=3):
    m = _mx(m)
    acc = None
    for piece in _split(x, parts):
        t = jnp.dot(m, piece, preferred_element_type=_F32)
        acc = t if acc is None else acc + t
    return acc


def _rms_norm(x, g, eps=NORM_EPS):
    return x * lax.rsqrt(jnp.mean(x * x, axis=-1, keepdims=True) + eps) * g


def _rms_norm_factors(x, g, eps=NORM_EPS):
    return _mx(x * g), lax.rsqrt(jnp.mean(x * x, axis=-1, keepdims=True) + eps)


def _sigmoid(x):
    return 0.5 * jnp.tanh(0.5 * x) + 0.5


def _softplus(x):
    return jnp.maximum(x, 0.0) + jnp.log(1.0 + jnp.exp(-jnp.abs(x)))


def _head_masks():
    lane = lax.broadcasted_iota(jnp.int32, (1, GROUP_W), 1)
    return [lax.shift_right_logical(lane, HEAD_SHIFT) == h for h in range(N_HEADS)]


def _stack_heads(x, hms):
    x = _mx(x)
    zero = jnp.zeros((), x.dtype)
    return jnp.concatenate([jnp.where(hm, x, zero) for hm in hms], axis=0)


def _mid_rows(b, n):
    c, gw = b.shape
    if n == c:
        return jnp.broadcast_to(b[n // 2 - 1:n // 2, :], b.shape)
    b3 = b.reshape(c // n, n, gw)
    return jnp.broadcast_to(b3[:, n // 2 - 1:n // 2, :], b3.shape).reshape(c, gw)


def _gla_chunk(q, k, v, g, sel, hms, xr, bd):
    c = CHUNK
    m = _dot_exact_lhs(sel, g)
    yield
    e2 = m[0:c]
    b = m[c:2 * c]
    f_b = jnp.exp(b)
    kv = jnp.where(bd, _dot_tn(v, k * jnp.exp(b[c - 1:c, :] - b)), 0.0)
    yield
    t_idx = lax.broadcasted_iota(jnp.int32, (c, GROUP_W), 0)
    n_piece = c // SUBLANES

    def pieces_of(x, rows_per_head):
        return [[x[h * rows_per_head + i * SUBLANES:h * rows_per_head + (i + 1) * SUBLANES]
                 for i in range(rows_per_head // SUBLANES)] for h in range(N_HEADS)]

    xr_p = [xr[i * SUBLANES:(i + 1) * SUBLANES] for i in range(n_piece)]
    s_0 = pieces_of(_dot_nt(_stack_heads(q, hms), k), c)
    diag = [xr_i < 1 for xr_i in xr_p]
    sc = [[jnp.where(diag[i], s_0[h][i], 0.0) for i in range(n_piece)] for h in range(N_HEADS)]
    yield
    for j in range(1, LEVELS + 1):
        n = 1 << j
        upper = (t_idx & (n - 1)) >= n // 2
        if j == 1:
            ex = jnp.where(upper, g, 0.0)
        elif j == 2:
            ex = e2
        else:
            bm = _mid_rows(b, n)
            ex = jnp.where(upper, b - bm, bm - b)
        f_j = jnp.exp(ex)
        kt = jnp.where(upper, 0.0, k * f_j)
        if n // 2 >= SUBLANES:
            up_idx = [i for i in range(n_piece) if (i * SUBLANES) % n >= n // 2]
            qt = jnp.concatenate([(q * f_j)[i * SUBLANES:(i + 1) * SUBLANES] for i in up_idx], axis=0)
        else:
            up_idx = list(range(n_piece))
            qt = jnp.where(upper, q * f_j, 0.0)
        s_j = pieces_of(_dot_nt(_stack_heads(qt, hms), kt), len(up_idx) * SUBLANES)
        yield
        same_block = {i: xr_p[i] < n for i in up_idx}
        for h in range(N_HEADS):
            for pos, i in enumerate(up_idx):
                add = s_j[h][pos] if j == LEVELS else jnp.where(same_block[i], s_j[h][pos], 0.0)
                sc[h][i] = sc[h][i] + add
    sc_row = jnp.concatenate([jnp.concatenate([sc[h][i] for h in range(N_HEADS)], axis=1)
                              for i in range(n_piece)], axis=0)
    o = _dot(sc_row, _stack_heads(v, hms))
    yield
    yield o, q * f_b, f_b[c - 1:c, :], kv


def _gla_state_step(pre, st):
    o_intra, qb, e_last, kv = pre
    return o_intra + _dot_nt(qb, st), st * e_last + kv


def _rwkv_chunk(r, k, v, al, be, lw, tri, hms, bd_c, bd_g, eye_g):
    c = CHUNK
    hc = N_HEADS * c
    gw = GROUP_W

    def bdiag(x_row):
        return jnp.where(bd_c, jnp.concatenate([_mx(x_row)] * N_HEADS, axis=0), jnp.zeros((), _MXU_DTYPE))

    cw = _dot_exact_lhs(tri, lw)
    yield
    cw_last = cw[c - 1:c, :]
    e_neg = jnp.exp(-cw)
    e_rem = jnp.exp(cw_last - cw)
    a_t = al * jnp.exp(cw - lw)
    r_t = r * jnp.exp(cw)
    prod = _dot_nt(jnp.concatenate([a_t, r_t], axis=0),
                   jnp.concatenate([_stack_heads(be * e_neg, hms), _stack_heads(k * e_neg, hms)], axis=0))
    yield
    t_r = lax.broadcasted_iota(jnp.int32, (c, hc), 0)
    s_r = lax.broadcasted_iota(jnp.int32, (c, hc), 1) & (c - 1)
    l_row = jnp.where(s_r < t_r, prod[0:c, 0:hc], 0.0)
    ak_row = jnp.where(s_r < t_r, prod[0:c, hc:2 * hc], 0.0)
    rb_row = jnp.where(s_r <= t_r, prod[c:2 * c, 0:hc], 0.0)
    rk_row = jnp.where(s_r <= t_r, prod[c:2 * c, hc:2 * hc], 0.0)

    t_row = jnp.where(s_r == t_r, 1.0, 0.0) - l_row
    p_row = _dot(l_row, bdiag(l_row))
    yield
    for i in range(LEVELS - 1):
        last = i == LEVELS - 2
        res = _dot(t_row if last else jnp.concatenate([p_row, t_row], axis=0), bdiag(p_row))
        yield
        if last:
            t_row = t_row + res
        else:
            p_row = res[0:c]
            t_row = t_row + res[c:2 * c]

    v_st = _stack_heads(v, hms)
    mv = _dot(ak_row, v_st)
    yield
    tm = _dot(t_row, jnp.concatenate([_stack_heads(a_t, hms), _stack_heads(mv, hms)], axis=1))
    yield
    ta = tm[:, 0:gw]
    uv = tm[:, gw:2 * gw]
    qe = r_t - _dot(rb_row, _stack_heads(ta, hms))
    yield
    ol = _dot(jnp.concatenate([rk_row, -rb_row], axis=1),
              jnp.concatenate([v_st, _stack_heads(uv, hms)], axis=0))
    yield
    b_r = be * e_rem
    gm = jnp.where(eye_g, jnp.exp(cw_last), 0.0) - jnp.where(bd_g, _dot_tn(b_r, ta), 0.0)
    yield
    hm = jnp.where(bd_g, _dot_tn(jnp.concatenate([k * e_rem, -b_r], axis=0),
                                 jnp.concatenate([v, uv], axis=0)), 0.0)
    yield
    yield gm, hm, qe, ol


def _rwkv_state_step(pre, p):
    gm, hm, qe, ol = pre
    gw = GROUP_W
    res = _dot(jnp.concatenate([gm, qe], axis=0), p)
    return res[gw:] + ol, res[0:gw] + hm


def _interleave(gens):
    results = [None] * len(gens)
    live = list(range(len(gens)))
    while live:
        for i in list(live):
            out = next(gens[i])
            if out is not None:
                results[i] = out
                live.remove(i)
    return results


def _mixer_kernel(p_ref, sel_ref, tri_ref, seg_ref, poolw_ref, w2_ref, a2_ref, g2_ref, glaw2_ref,
                  mu_ref, vec_ref, lb_ref, y_ref,
                  feat, osc, aux, pool_prev, rw_prev, hg_st, rw_p, gl_st, *, layer):
    tb = TIME_TILE
    gw = GROUP_W
    sub = SUB_ROWS
    n_sub = tb // sub
    c = CHUNK
    hc = N_HEADS * c
    ti = pl.program_id(1)

    @pl.when(ti == 0)
    def _():
        pool_prev[...] = jnp.zeros_like(pool_prev)
        rw_prev[...] = jnp.zeros_like(rw_prev)
        hg_st[...] = jnp.zeros_like(hg_st)
        rw_p[...] = jnp.zeros_like(rw_p)
        gl_st[...] = jnp.zeros_like(gl_st)

    def vec(i):
        return vec_ref[i:i + 1, :]

    seg = seg_ref[...]
    hms = _head_masks()

    def features(r0):
        def pcol(base, i, n=1):
            return p_ref[0, r0:r0 + sub, base + i * gw: base + (i + n) * gw]

        def put(i, val):
            feat[r0:r0 + sub, i * gw:(i + 1) * gw] = val

        p_pool = pcol(_P_POOL, 0)
        before = pool_prev[...] if r0 == 0 else p_ref[0, r0 - MAX_WINDOW:r0, _P_POOL:_P_POOL + gw]
        ext = jnp.concatenate([before, p_pool], axis=0)
        if r0 + sub == tb:
            pool_prev[...] = p_pool[sub - MAX_WINDOW:, :]
        lane = lax.broadcasted_iota(jnp.int32, (sub, gw), 1)
        grp = lax.shift_right_logical(lane, HEAD_SHIFT)
        tpos = lax.broadcasted_iota(jnp.int32, (sub, gw), 0) + (ti * tb + r0)
        win = ext
        win_sum = jnp.zeros((sub, gw), _F32)
        count = jnp.zeros((sub, gw), _F32)
        for gi, w in enumerate(POOL_WINDOWS):
            sh = w // 2
            while sh < w:
                win = win + pltpu.roll(win, sh, 0)
                sh *= 2
            win_sum = jnp.where(grp == gi, win[MAX_WINDOW:, :], win_sum)
            count = jnp.where(grp == gi, jnp.minimum(tpos + 1, w).astype(_F32), count)
        pooled = win_sum / count - p_pool
        y_pool = (_dot(pooled, poolw_ref[...]) + vec(_V_POOL_B)) * vec(_V_POOL_SCALE)
        y_ref[0, r0:r0 + sub, 0:gw] = y_pool.astype(y_ref.dtype)
        yield

        sm = jnp.exp(lb_ref[...] - jnp.max(lb_ref[...], axis=0, keepdims=True))
        sm = sm / jnp.sum(sm, axis=0, keepdims=True)
        lb = jnp.sum(sm[0:layer + 1], axis=0, keepdims=True) - sm[0:1]
        hq = pcol(_P_HG, 0)
        f = lb + (1.0 - lb) * _sigmoid(pcol(_P_HG, 1))
        put(_F_HQ, hq * _sigmoid(hq) * QK_SCALE)
        put(_F_HK, 1.0 - f)
        put(_F_HV, pcol(_P_HG, 2))
        put(_F_HGD, jnp.log(jnp.maximum(f, GATE_FLOOR)))
        yield

        p_rw = pcol(_P_RW, 0, 4)
        before = rw_prev[...] if r0 == 0 else p_ref[0, r0 - 1:r0, _P_RW:_P_RW + 4 * gw]
        row0 = lax.broadcasted_iota(jnp.int32, p_rw.shape, 0) == 0
        prev = jnp.where(row0, before, pltpu.roll(p_rw, 1, 0))
        if r0 + sub == tb:
            rw_prev[...] = p_rw[sub - 1:sub, :]
        p_rw = p_rw + (prev - p_rw) * mu_ref[...]
        rr = p_rw[:, 0:gw]
        rk = p_rw[:, gw:2 * gw]
        rv = p_rw[:, 2 * gw:3 * gw]
        xwa = p_rw[:, 3 * gw:3 * gw + LANES]
        xg = p_rw[:, 3 * gw + LANES:4 * gw]
        w_log = -_softplus(-(vec(_V_W0) + _dot(jnp.tanh(xwa), w2_ref[...]))) - 0.5
        yield
        a = _sigmoid(vec(_V_A0) + _dot(xwa, a2_ref[...]))
        yield
        aux[r0:r0 + sub, gw:2 * gw] = _dot(_sigmoid(xg), g2_ref[...])
        yield
        kk = rk * vec(_V_KK)
        kk = kk / jnp.maximum(jnp.sqrt(_dot(kk * kk, seg) * HEAD_DIM), 1e-12)
        yield
        rk = rk * (1.0 + (a - 1.0) * vec(_V_KA))
        put(_F_RR, rr)
        put(_F_RK, rk)
        put(_F_RV, rv)
        put(_F_RA, kk)
        put(_F_RB, kk * a)
        put(_F_RW, -jnp.exp(w_log))
        aux[r0:r0 + sub, 0:gw] = _dot(rr * rk * vec(_V_RK), seg) * HEAD_DIM * rv
        yield

        ga = p_ref[0, r0:r0 + sub, _P_GLA + 4 * gw:_P_GLA + 4 * gw + LANES]
        z = _dot(ga, glaw2_ref[...]) + vec(_V_GLA_B)
        put(_F_GQ, pcol(_P_GLA, 0) * QK_SCALE)
        put(_F_GK, pcol(_P_GLA, 1))
        put(_F_GV, pcol(_P_GLA, 2))
        put(_F_GGD, -_softplus(-z) / GLA_GATE_TAU)
        yield
        yield True

    def outputs(r0):
        def pcol(base, i):
            return p_ref[0, r0:r0 + sub, base + i * gw: base + (i + 1) * gw]

        o_h = osc[r0:r0 + sub, 0:gw]
        y_hg = _rms_norm(o_h, vec(_V_HGRN_NORM)) * _sigmoid(pcol(_P_HG, 3))
        y_ref[0, r0:r0 + sub, gw:2 * gw] = y_hg.astype(y_ref.dtype)
        yield
        o_r = osc[r0:r0 + sub, gw:2 * gw]
        cen = o_r - _dot(o_r, seg)
        yield
        var = _dot(cen * cen, seg)
        gn = cen * lax.rsqrt(var + RWKV_GN_EPS) * vec(_V_LNW) + vec(_V_LNB)
        y_rw = (gn + aux[r0:r0 + sub, 0:gw]) * aux[r0:r0 + sub, gw:2 * gw]
        y_ref[0, r0:r0 + sub, 2 * gw:3 * gw] = y_rw.astype(y_ref.dtype)
        yield
        o_g = osc[r0:r0 + sub, 2 * gw:3 * gw]
        ms = _dot(o_g * o_g, seg)
        gg = pcol(_P_GLA, 3)
        y_gl = o_g * lax.rsqrt(ms + NORM_EPS) * vec(_V_GLA_NORM) * (gg * _sigmoid(gg))
        y_ref[0, r0:r0 + sub, 3 * gw:4 * gw] = y_gl.astype(y_ref.dtype)
        yield
        yield True

    xr = lax.broadcasted_iota(jnp.int32, (c, c), 0) ^ lax.broadcasted_iota(jnp.int32, (c, c), 1)
    bd_c = (lax.shift_right_logical(lax.broadcasted_iota(jnp.int32, (hc, hc), 0), LEVELS)
            == lax.shift_right_logical(lax.broadcasted_iota(jnp.int32, (hc, hc), 1), LEVELS))
    row_g = lax.broadcasted_iota(jnp.int32, (gw, gw), 0)
    col_g = lax.broadcasted_iota(jnp.int32, (gw, gw), 1)
    bd_g = lax.shift_right_logical(row_g, HEAD_SHIFT) == lax.shift_right_logical(col_g, HEAD_SHIFT)
    eye_g = row_g == col_g
    sel = sel_ref[...]
    tri = tri_ref[...]

    _interleave([features(0)])
    p_rw_st, st_h, st_g = rw_p[...], hg_st[...], gl_st[...]
    for s_i in range(n_sub):
        gens = []
        for u in range(CHUNK_UNROLL):
            r0 = s_i * sub + u * c

            def ft(i, r0=r0):
                return feat[r0:r0 + c, i * gw:(i + 1) * gw]

            gens += [
                _rwkv_chunk(ft(_F_RR), ft(_F_RK), ft(_F_RV), ft(_F_RA), ft(_F_RB), ft(_F_RW),
                            tri, hms, bd_c, bd_g, eye_g),
                _gla_chunk(ft(_F_HQ), ft(_F_HK), ft(_F_HV), ft(_F_HGD), sel, hms, xr, bd_g),
                _gla_chunk(ft(_F_GQ), ft(_F_GK), ft(_F_GV), ft(_F_GGD), sel, hms, xr, bd_g)]
        if s_i + 1 < n_sub:
            gens.append(features((s_i + 1) * sub))
        if s_i >= 1:
            gens.append(outputs((s_i - 1) * sub))
        pre = _interleave(gens)
        for u in range(CHUNK_UNROLL):
            r0 = s_i * sub + u * c
            o_r, p_rw_st = _rwkv_state_step(pre[3 * u], p_rw_st)
            o_h, st_h = _gla_state_step(pre[3 * u + 1], st_h)
            o_g, st_g = _gla_state_step(pre[3 * u + 2], st_g)
            osc[r0:r0 + c, 0:gw] = o_h
            osc[r0:r0 + c, gw:2 * gw] = o_r
            osc[r0:r0 + c, 2 * gw:3 * gw] = o_g
    rw_p[...] = p_rw_st
    hg_st[...] = st_h
    gl_st[...] = st_g
    _interleave([outputs((n_sub - 1) * sub)])


def _selection_matrix():
    c = CHUNK
    m = np.zeros((2 * c, c), np.float32)
    for t in range(c):
        mid = (t // 4) * 4 + 1
        if t % 4 >= 2:
            m[t, mid + 1:t + 1] = 1.0
        else:
            m[t, t + 1:mid + 1] = 1.0
        m[c + t, :t + 1] = 1.0
    return m


def _const_spec(shape):
    nd = len(shape)
    return pl.BlockSpec(shape, lambda *_: (0,) * nd, pipeline_mode=pl.Buffered(1))


def _layer_spec(shape, layer):
    return pl.BlockSpec((None,) + tuple(shape[1:]), lambda *_: (layer, 0, 0), pipeline_mode=pl.Buffered(1))


def _mixer_call(p, consts, layer):
    b, t, _ = p.shape
    tb = TIME_TILE
    gw = GROUP_W
    in_specs = [pl.BlockSpec((1, tb, D_IN_PAD), lambda bi, ti: (bi, ti, 0))]
    in_specs += [_const_spec(a.shape) for a in consts]
    return pl.pallas_call(
        functools.partial(_mixer_kernel, layer=layer),
        grid=(b, t // tb),
        in_specs=in_specs,
        out_specs=pl.BlockSpec((1, tb, D_MODEL), lambda bi, ti: (bi, ti, 0)),
        out_shape=jax.ShapeDtypeStruct((b, t, D_MODEL), _MXU_DTYPE),
        scratch_shapes=[
            pltpu.VMEM((tb, _N_FEAT * gw), _F32),
            pltpu.VMEM((tb, 3 * gw), _F32),
            pltpu.VMEM((tb, 2 * gw), _F32),
            pltpu.VMEM((MAX_WINDOW, gw), _F32),
            pltpu.VMEM((1, 4 * gw), _F32),
            pltpu.VMEM((gw, gw), _F32),
            pltpu.VMEM((gw, gw), _F32),
            pltpu.VMEM((gw, gw), _F32),
        ],
        compiler_params=pltpu.CompilerParams(
            dimension_semantics=("arbitrary", "arbitrary"), vmem_limit_bytes=VMEM_LIMIT_BYTES),
        name=f"mixer_l{layer}",
    )(p, *consts)


def _ffn_kernel(*refs, final, fused_proj):
    if fused_proj:
        x_ref, y_ref, wp_ref, g_ref, win_ref, wout_ref, gf_ref, o_ref = refs
        x = x_ref[...] + jnp.dot(y_ref[...], wp_ref[...], preferred_element_type=_F32)
    else:
        x_ref, g_ref, win_ref, wout_ref, gf_ref, o_ref = refs
        x = x_ref[...]
    xg, rs = _rms_norm_factors(x, g_ref[...])
    acc = jnp.zeros(x.shape, _F32)
    lo = 0
    for width in FF_BLOCKS:
        gate = jnp.dot(xg, win_ref[:, lo:lo + width], preferred_element_type=_F32) * rs
        up = jnp.dot(xg, win_ref[:, D_FF + lo:D_FF + lo + width], preferred_element_type=_F32) * rs
        act = _mx(gate * _sigmoid(gate) * up)
        acc = acc + jnp.dot(act, wout_ref[lo:lo + width, :], preferred_element_type=_F32)
        lo += width
    y = x + 0.5 * acc
    if final:
        y = _rms_norm(y, gf_ref[...])
    o_ref[...] = y


def _ffn_call(x2, g, w_in, w_out, g_final, final, layer, proj=None):
    n, d = x2.shape
    tm = ROW_TILE
    row_spec = pl.BlockSpec((tm, d), lambda i: (i, 0))
    args = [x2]
    in_specs = [row_spec]
    if proj is not None:
        y2, w_proj = proj
        args += [y2, w_proj]
        in_specs += [pl.BlockSpec((tm, y2.shape[1]), lambda i: (i, 0)), _layer_spec(w_proj.shape, layer)]
    args += [g, w_in, w_out, g_final]
    in_specs += [_const_spec(g.shape), _layer_spec(w_in.shape, layer), _layer_spec(w_out.shape, layer),
                 _const_spec(g_final.shape)]
    return pl.pallas_call(
        functools.partial(_ffn_kernel, final=final, fused_proj=proj is not None),
        grid=(n // tm,),
        in_specs=in_specs,
        out_specs=row_spec,
        out_shape=jax.ShapeDtypeStruct((n, d), _F32),
        compiler_params=pltpu.CompilerParams(
            dimension_semantics=("arbitrary",), vmem_limit_bytes=VMEM_LIMIT_BYTES),
        name="ffn_proj" if proj is not None else "ffn",
    )(*args)


def _inproj_kernel(x_ref, g_ref, w_ref, o_ref):
    xg, rs = _rms_norm_factors(x_ref[...], g_ref[...])
    o_ref[...] = jnp.dot(xg, w_ref[...], preferred_element_type=_F32) * rs


def _inproj_call(x2, g, w, layer):
    n, d = x2.shape
    tm = ROW_TILE
    return pl.pallas_call(
        _inproj_kernel,
        grid=(n // tm,),
        in_specs=[pl.BlockSpec((tm, d), lambda i: (i, 0)), _const_spec(g.shape), _layer_spec(w.shape, layer)],
        out_specs=pl.BlockSpec((tm, w.shape[2]), lambda i: (i, 0)),
        out_shape=jax.ShapeDtypeStruct((n, w.shape[2]), _F32),
        compiler_params=pltpu.CompilerParams(
            dimension_semantics=("arbitrary",), vmem_limit_bytes=VMEM_LIMIT_BYTES),
        name="inproj",
    )(x2, g, w)


def _pad_rows(w, rows):
    return jnp.pad(w, ((0, rows - w.shape[0]), (0, 0)))


def kernel(x, norm_ffn1, ffn1_w_in, ffn1_w_out, norm_mix, w_in, w_out, pool_w, pool_b, pool_scale,
           hgrn_lb_logits, hgrn_norm, rwkv_mu, rwkv_w0, rwkv_w2, rwkv_a0, rwkv_a2, rwkv_g2, rwkv_k_k,
           rwkv_k_a, rwkv_r_k, rwkv_ln_w, rwkv_ln_b, gla_w2, gla_b, gla_norm, norm_ffn2, ffn2_w_in,
           ffn2_w_out, norm_final):
    b, t, d = x.shape
    depth = norm_ffn1.shape[0]
    gw = GROUP_W
    assert d == D_MODEL and t % TIME_TILE == 0 and (b * t) % ROW_TILE == 0
    assert TIME_TILE % (CHUNK * CHUNK_UNROLL) == 0
    assert sum(FF_BLOCKS) == D_FF and all(w % MXU_TILE == 0 for w in FF_BLOCKS)

    sel = jnp.asarray(_selection_matrix(), _MXU_DTYPE)
    tri = jnp.asarray(np.tril(np.ones((CHUNK, CHUNK), np.float32)), _MXU_DTYPE)
    head = np.arange(gw) // HEAD_DIM
    seg = jnp.asarray((head[:, None] == head[None, :]).astype(np.float32) / HEAD_DIM, _MXU_DTYPE)
    bdmask = jnp.asarray((head[:, None] == head[None, :]).astype(np.float32))
    gf = norm_final.reshape(1, d)

    w1_in, w1_out, w2_in, w2_out, w_mix_out = (_mx(w) for w in (ffn1_w_in, ffn1_w_out, ffn2_w_in, ffn2_w_out, w_out))
    w_mix_in = _mx(jnp.pad(w_in, ((0, 0), (0, 0), (0, D_IN_PAD - D_IN))))

    h = x.reshape(b * t, d)
    for l in range(depth):
        h = _ffn_call(h, norm_ffn1[l].reshape(1, d), w1_in, w1_out, gf, False, l)

        p = _inproj_call(h, norm_mix[l].reshape(1, d), w_mix_in, l)
        poolw = _mx(jnp.tile(pool_w[l].reshape(gw, POOL_CH), (1, len(POOL_WINDOWS))) * bdmask)
        w2p = _mx(_pad_rows(rwkv_w2[l], LANES))
        a2p = _mx(jnp.pad(rwkv_a2[l], ((RWKV_DECAY_LORA, 0), (0, 0))))
        vecs = jnp.stack([pool_b[l], pool_scale[l], hgrn_norm[l], rwkv_w0[l], rwkv_a0[l], rwkv_k_k[l],
                          rwkv_k_a[l], rwkv_r_k[l], rwkv_ln_w[l], rwkv_ln_b[l], gla_b[l], gla_norm[l]])
        consts = (sel, tri, seg, poolw, w2p, a2p, _mx(rwkv_g2[l]), _mx(_pad_rows(gla_w2[l], LANES)),
                  rwkv_mu[l].reshape(1, 4 * gw), _pad_rows(vecs, _N_VEC), hgrn_lb_logits)
        y = _mixer_call(p.reshape(b, t, D_IN_PAD), consts, l)
        h = _ffn_call(h, norm_ffn2[l].reshape(1, d), w2_in, w2_out, gf, l == depth - 1, l,
                      proj=(y.reshape(b * t, d), w_mix_out))
    return h.reshape(b, t, d)
```

```python
import functools

import numpy as np
import jax
import jax.numpy as jnp
from jax import lax
from jax.experimental import pallas as pl
from jax.experimental.pallas import tpu as pltpu

D_MODEL = 1024
GROUP_W = 256
HEAD_DIM = 64
N_HEADS = GROUP_W // HEAD_DIM
HEAD_SHIFT = HEAD_DIM.bit_length() - 1
QK_SCALE = HEAD_DIM ** -0.5
POOL_WINDOWS = (2, 4, 8, 16)
POOL_CH = GROUP_W // len(POOL_WINDOWS)
MAX_WINDOW = 16
RWKV_DECAY_LORA = 64
RWKV_A_LORA = 64
RWKV_GATE_LORA = 128
RWKV_GN_EPS = 64e-5
GLA_GATE_LORA = 16
GLA_GATE_TAU = 16.0
D_IN = 13 * GROUP_W + GLA_GATE_LORA
D_FF = 2816
NORM_EPS = 1e-6
GATE_FLOOR = 1e-30

LANES = 128
SUBLANES = 8
VMEM_LIMIT_BYTES = 56 * 1024 * 1024

D_IN_PAD = 13 * GROUP_W + LANES
CHUNK = 64
LEVELS = CHUNK.bit_length() - 1
TIME_TILE = 512
CHUNK_UNROLL = 2
SUB_ROWS = CHUNK * CHUNK_UNROLL
GLA_STAGE_PERIOD = 2
INVERSE_BASE = 8
ROW_TILE = 512
MXU_TILE = 256
FF_BLOCKS = (1536, 1280)

_P_POOL = 0
_P_HG = GROUP_W
_P_RW = 5 * GROUP_W
_P_GLA = 9 * GROUP_W

_F_HQ, _F_HK, _F_HV, _F_HGD = 0, 1, 2, 3
_F_RR, _F_RK, _F_RV, _F_RA, _F_RB, _F_RW = 4, 5, 6, 7, 8, 9
_F_GQ, _F_GK, _F_GV, _F_GGD = 10, 11, 12, 13
_N_FEAT = 14

(_V_POOL_B, _V_POOL_SCALE, _V_HGRN_NORM, _V_W0, _V_A0, _V_KK, _V_KA, _V_RK, _V_LNW, _V_LNB,
 _V_GLA_B, _V_GLA_NORM) = range(12)
_N_VEC = 16

_MXU_DTYPE = jnp.bfloat16
_F32 = jnp.float32


def _mx(a):
    return a.astype(_MXU_DTYPE)


def _dot(a, b):
    return jnp.dot(_mx(a), _mx(b), preferred_element_type=_F32)


def _dot_nt(a, b):
    return lax.dot_general(_mx(a), _mx(b), (((1,), (1,)), ((), ())), preferred_element_type=_F32)


def _dot_tn(a, b):
    return lax.dot_general(_mx(a), _mx(b), (((0,), (0,)), ((), ())), preferred_element_type=_F32)


def _split(x, parts):
    out = []
    r = x
    for _ in range(parts - 1):
        h = r.astype(_MXU_DTYPE)
        out.append(h)
        r = r - h.astype(_F32)
    out.append(r.astype(_MXU_DTYPE))
    return out


def _dot_exact_lhs(m, x, parts=3):
    m = _mx(m)
    acc = None
    for piece in _split(x, parts):
        t = jnp.dot(m, piece, preferred_element_type=_F32)
        acc = t if acc is None else acc + t
    return acc


def _rms_norm(x, g, eps=NORM_EPS):
    return x * lax.rsqrt(jnp.mean(x * x, axis=-1, keepdims=True) + eps) * g


def _rms_norm_factors(x, g, eps=NORM_EPS):
    return _mx(x * g), lax.rsqrt(jnp.mean(x * x, axis=-1, keepdims=True) + eps)


def _sigmoid(x):
    return 0.5 * jnp.tanh(0.5 * x) + 0.5


def _softplus(x):
    return jnp.maximum(x, 0.0) + jnp.log(1.0 + jnp.exp(-jnp.abs(x)))


def _head_masks():
    lane = lax.broadcasted_iota(jnp.int32, (1, GROUP_W), 1)
    return [lax.shift_right_logical(lane, HEAD_SHIFT) == h for h in range(N_HEADS)]


def _stack_heads(x, hms):
    x = _mx(x)
    zero = jnp.zeros((), x.dtype)
    return jnp.concatenate([jnp.where(hm, x, zero) for hm in hms], axis=0)


def _mid_rows(b, n):
    c, gw = b.shape
    if n == c:
        return jnp.broadcast_to(b[n // 2 - 1:n // 2, :], b.shape)
    b3 = b.reshape(c // n, n, gw)
    return jnp.broadcast_to(b3[:, n // 2 - 1:n // 2, :], b3.shape).reshape(c, gw)


def _gla_chunk(q, k, v, g, sel, hms, xr, bd):
    c = CHUNK
    m = _dot_exact_lhs(sel, g)
    yield
    e2 = m[0:c]
    b = m[c:2 * c]
    f_b = jnp.exp(b)
    kv = jnp.where(bd, _dot_tn(v, k * jnp.exp(b[c - 1:c, :] - b)), 0.0)
    yield
    t_idx = lax.broadcasted_iota(jnp.int32, (c, GROUP_W), 0)
    n_piece = c // SUBLANES

    def pieces_of(x, rows_per_head):
        return [[x[h * rows_per_head + i * SUBLANES:h * rows_per_head + (i + 1) * SUBLANES]
                 for i in range(rows_per_head // SUBLANES)] for h in range(N_HEADS)]

    xr_p = [xr[i * SUBLANES:(i + 1) * SUBLANES] for i in range(n_piece)]
    s_0 = pieces_of(_dot_nt(_stack_heads(q, hms), k), c)
    diag = [xr_i < 1 for xr_i in xr_p]
    sc = [[jnp.where(diag[i], s_0[h][i], 0.0) for i in range(n_piece)] for h in range(N_HEADS)]
    yield
    for j in range(1, LEVELS + 1):
        n = 1 << j
        upper = (t_idx & (n - 1)) >= n // 2
        if j == 1:
            ex = jnp.where(upper, g, 0.0)
        elif j == 2:
            ex = e2
        else:
            bm = _mid_rows(b, n)
            ex = jnp.where(upper, b - bm, bm - b)
        f_j = jnp.exp(ex)
        kt = jnp.where(upper, 0.0, k * f_j)
        if n // 2 >= SUBLANES:
            up_idx = [i for i in range(n_piece) if (i * SUBLANES) % n >= n // 2]
            qt = jnp.concatenate([(q * f_j)[i * SUBLANES:(i + 1) * SUBLANES] for i in up_idx], axis=0)
        else:
            up_idx = list(range(n_piece))
            qt = jnp.where(upper, q * f_j, 0.0)
        s_j = pieces_of(_dot_nt(_stack_heads(qt, hms), kt), len(up_idx) * SUBLANES)
        yield
        same_block = {i: xr_p[i] < n for i in up_idx}
        for h in range(N_HEADS):
            for pos, i in enumerate(up_idx):
                add = s_j[h][pos] if j == LEVELS else jnp.where(same_block[i], s_j[h][pos], 0.0)
                sc[h][i] = sc[h][i] + add
    sc_row = jnp.concatenate([jnp.concatenate([sc[h][i] for h in range(N_HEADS)], axis=1)
                              for i in range(n_piece)], axis=0)
    o = _dot(sc_row, _stack_heads(v, hms))
    yield
    yield o, q * f_b, f_b[c - 1:c, :], kv


def _gla_state_step(pre, st):
    o_intra, qb, e_last, kv = pre
    return o_intra + _dot_nt(qb, st), st * e_last + kv


def _rwkv_chunk(r, k, v, al, be, lw, tri, hms, bd_c, bd_g, eye_g, inv_masks):
    c = CHUNK
    hc = N_HEADS * c
    gw = GROUP_W

    def bdiag(x_row):
        return jnp.where(bd_c, jnp.concatenate([_mx(x_row)] * N_HEADS, axis=0), jnp.zeros((), _MXU_DTYPE))

    cw = _dot_exact_lhs(tri, lw)
    yield
    cw_last = cw[c - 1:c, :]
    e_neg = jnp.exp(-cw)
    e_rem = jnp.exp(cw_last - cw)
    a_t = al * jnp.exp(cw - lw)
    r_t = r * jnp.exp(cw)
    prod = _dot_nt(jnp.concatenate([a_t, r_t], axis=0),
                   jnp.concatenate([_stack_heads(be * e_neg, hms), _stack_heads(k * e_neg, hms)], axis=0))
    yield
    t_r = lax.broadcasted_iota(jnp.int32, (c, hc), 0)
    s_r = lax.broadcasted_iota(jnp.int32, (c, hc), 1) & (c - 1)
    l_row = jnp.where(s_r < t_r, prod[0:c, 0:hc], 0.0)
    ak_row = jnp.where(s_r < t_r, prod[0:c, hc:2 * hc], 0.0)
    rb_row = jnp.where(s_r <= t_r, prod[c:2 * c, 0:hc], 0.0)
    rk_row = jnp.where(s_r <= t_r, prod[c:2 * c, hc:2 * hc], 0.0)

    l_tiled = jnp.concatenate([_mx(l_row)] * N_HEADS, axis=0)
    zero = jnp.zeros((), _MXU_DTYPE)
    t_row = jnp.where(s_r == t_r, 1.0, 0.0) - jnp.where(inv_masks["base_row"], l_row, 0.0)
    l_b = jnp.where(inv_masks["base_bd"], l_tiled, zero)
    p_row = _dot(jnp.where(inv_masks["base_row"], l_row, 0.0), l_b)
    yield
    power = 2
    while True:
        last = 2 * power >= INVERSE_BASE
        res = _dot(t_row if last else jnp.concatenate([p_row, t_row], axis=0), bdiag(p_row))
        yield
        if last:
            t_row = t_row + res
            break
        p_row = res[0:c]
        t_row = t_row + res[c:2 * c]
        power *= 2
    for off_bd in inv_masks["merge_bd"]:
        x = _dot(t_row, jnp.where(off_bd, l_tiled, zero))
        yield
        t_row = t_row - _dot(x, bdiag(t_row))
        yield

    v_st = _stack_heads(v, hms)
    mv = _dot(ak_row, v_st)
    yield
    tm = _dot(t_row, jnp.concatenate([_stack_heads(a_t, hms), _stack_heads(mv, hms)], axis=1))
    yield
    ta = tm[:, 0:gw]
    uv = tm[:, gw:2 * gw]
    qe = r_t - _dot(rb_row, _stack_heads(ta, hms))
    yield
    ol = _dot(jnp.concatenate([rk_row, -rb_row], axis=1),
              jnp.concatenate([v_st, _stack_heads(uv, hms)], axis=0))
    yield
    b_r = be * e_rem
    gm = jnp.where(eye_g, jnp.exp(cw_last), 0.0) - jnp.where(bd_g, _dot_tn(b_r, ta), 0.0)
    yield
    hm = jnp.where(bd_g, _dot_tn(jnp.concatenate([k * e_rem, -b_r], axis=0),
                                 jnp.concatenate([v, uv], axis=0)), 0.0)
    yield
    yield gm, hm, qe, ol


def _rwkv_state_step(pre, p):
    gm, hm, qe, ol = pre
    gw = GROUP_W
    res = _dot(jnp.concatenate([gm, qe], axis=0), p)
    return res[gw:] + ol, res[0:gw] + hm


def _interleave(gens, periods=None):
    results = [None] * len(gens)
    live = list(range(len(gens)))
    periods = periods or [1] * len(gens)
    rnd = 0
    while live:
        for i in list(live):
            if rnd % periods[i] != i % periods[i]:
                continue
            out = next(gens[i])
            if out is not None:
                results[i] = out
                live.remove(i)
        rnd += 1
    return results


def _mixer_kernel(p_ref, sel_ref, tri_ref, seg_ref, poolw_ref, w2_ref, a2_ref, g2_ref, glaw2_ref,
                  mu_ref, vec_ref, lb_ref, y_ref,
                  feat, osc, aux, pool_prev, rw_prev, hg_st, rw_p, gl_st, *, layer):
    tb = TIME_TILE
    gw = GROUP_W
    sub = SUB_ROWS
    n_sub = tb // sub
    c = CHUNK
    hc = N_HEADS * c
    ti = pl.program_id(1)

    @pl.when(ti == 0)
    def _():
        pool_prev[...] = jnp.zeros_like(pool_prev)
        rw_prev[...] = jnp.zeros_like(rw_prev)
        hg_st[...] = jnp.zeros_like(hg_st)
        rw_p[...] = jnp.zeros_like(rw_p)
        gl_st[...] = jnp.zeros_like(gl_st)

    def vec(i):
        return vec_ref[i:i + 1, :]

    seg = seg_ref[...]
    hms = _head_masks()

    def features(r0):
        def pcol(base, i, n=1):
            return p_ref[0, r0:r0 + sub, base + i * gw: base + (i + n) * gw]

        def put(i, val):
            feat[r0:r0 + sub, i * gw:(i + 1) * gw] = val

        p_pool = pcol(_P_POOL, 0)
        before = pool_prev[...] if r0 == 0 else p_ref[0, r0 - MAX_WINDOW:r0, _P_POOL:_P_POOL + gw]
        ext = jnp.concatenate([before, p_pool], axis=0)
        if r0 + sub == tb:
            pool_prev[...] = p_pool[sub - MAX_WINDOW:, :]
        lane = lax.broadcasted_iota(jnp.int32, (sub, gw), 1)
        grp = lax.shift_right_logical(lane, HEAD_SHIFT)
        tpos = lax.broadcasted_iota(jnp.int32, (sub, gw), 0) + (ti * tb + r0)
        win = ext
        win_sum = jnp.zeros((sub, gw), _F32)
        count = jnp.zeros((sub, gw), _F32)
        for gi, w in enumerate(POOL_WINDOWS):
            sh = w // 2
            while sh < w:
                win = win + pltpu.roll(win, sh, 0)
                sh *= 2
            win_sum = jnp.where(grp == gi, win[MAX_WINDOW:, :], win_sum)
            count = jnp.where(grp == gi, jnp.minimum(tpos + 1, w).astype(_F32), count)
        pooled = win_sum / count - p_pool
        y_pool = (_dot(pooled, poolw_ref[...]) + vec(_V_POOL_B)) * vec(_V_POOL_SCALE)
        y_ref[0, r0:r0 + sub, 0:gw] = y_pool.astype(y_ref.dtype)
        yield

        sm = jnp.exp(lb_ref[...] - jnp.max(lb_ref[...], axis=0, keepdims=True))
        sm = sm / jnp.sum(sm, axis=0, keepdims=True)
        lb = jnp.sum(sm[0:layer + 1], axis=0, keepdims=True) - sm[0:1]
        hq = pcol(_P_HG, 0)
        f = lb + (1.0 - lb) * _sigmoid(pcol(_P_HG, 1))
        put(_F_HQ, hq * _sigmoid(hq) * QK_SCALE)
        put(_F_HK, 1.0 - f)
        put(_F_HV, pcol(_P_HG, 2))
        put(_F_HGD, jnp.log(jnp.maximum(f, GATE_FLOOR)))
        yield

        p_rw = pcol(_P_RW, 0, 4)
        before = rw_prev[...] if r0 == 0 else p_ref[0, r0 - 1:r0, _P_RW:_P_RW + 4 * gw]
        row0 = lax.broadcasted_iota(jnp.int32, p_rw.shape, 0) == 0
        prev = jnp.where(row0, before, pltpu.roll(p_rw, 1, 0))
        if r0 + sub == tb:
            rw_prev[...] = p_rw[sub - 1:sub, :]
        p_rw = p_rw + (prev - p_rw) * mu_ref[...]
        rr = p_rw[:, 0:gw]
        rk = p_rw[:, gw:2 * gw]
        rv = p_rw[:, 2 * gw:3 * gw]
        xwa = p_rw[:, 3 * gw:3 * gw + LANES]
        xg = p_rw[:, 3 * gw + LANES:4 * gw]
        w_log = -_softplus(-(vec(_V_W0) + _dot(jnp.tanh(xwa), w2_ref[...]))) - 0.5
        yield
        a = _sigmoid(vec(_V_A0) + _dot(xwa, a2_ref[...]))
        yield
        aux[r0:r0 + sub, gw:2 * gw] = _dot(_sigmoid(xg), g2_ref[...])
        yield
        kk = rk * vec(_V_KK)
        kk = kk / jnp.maximum(jnp.sqrt(_dot(kk * kk, seg) * HEAD_DIM), 1e-12)
        yield
        rk = rk * (1.0 + (a - 1.0) * vec(_V_KA))
        put(_F_RR, rr)
        put(_F_RK, rk)
        put(_F_RV, rv)
        put(_F_RA, kk)
        put(_F_RB, kk * a)
        put(_F_RW, -jnp.exp(w_log))
        aux[r0:r0 + sub, 0:gw] = _dot(rr * rk * vec(_V_RK), seg) * HEAD_DIM * rv
        yield

        ga = p_ref[0, r0:r0 + sub, _P_GLA + 4 * gw:_P_GLA + 4 * gw + LANES]
        z = _dot(ga, glaw2_ref[...]) + vec(_V_GLA_B)
        put(_F_GQ, pcol(_P_GLA, 0) * QK_SCALE)
        put(_F_GK, pcol(_P_GLA, 1))
        put(_F_GV, pcol(_P_GLA, 2))
        put(_F_GGD, -_softplus(-z) / GLA_GATE_TAU)
        yield
        yield True

    def outputs(r0):
        def pcol(base, i):
            return p_ref[0, r0:r0 + sub, base + i * gw: base + (i + 1) * gw]

        o_h = osc[r0:r0 + sub, 0:gw]
        y_hg = _rms_norm(o_h, vec(_V_HGRN_NORM)) * _sigmoid(pcol(_P_HG, 3))
        y_ref[0, r0:r0 + sub, gw:2 * gw] = y_hg.astype(y_ref.dtype)
        yield
        o_r = osc[r0:r0 + sub, gw:2 * gw]
        cen = o_r - _dot(o_r, seg)
        yield
        var = _dot(cen * cen, seg)
        gn = cen * lax.rsqrt(var + RWKV_GN_EPS) * vec(_V_LNW) + vec(_V_LNB)
        y_rw = (gn + aux[r0:r0 + sub, 0:gw]) * aux[r0:r0 + sub, gw:2 * gw]
        y_ref[0, r0:r0 + sub, 2 * gw:3 * gw] = y_rw.astype(y_ref.dtype)
        yield
        o_g = osc[r0:r0 + sub, 2 * gw:3 * gw]
        ms = _dot(o_g * o_g, seg)
        gg = pcol(_P_GLA, 3)
        y_gl = o_g * lax.rsqrt(ms + NORM_EPS) * vec(_V_GLA_NORM) * (gg * _sigmoid(gg))
        y_ref[0, r0:r0 + sub, 3 * gw:4 * gw] = y_gl.astype(y_ref.dtype)
        yield
        yield True

    xr = lax.broadcasted_iota(jnp.int32, (c, c), 0) ^ lax.broadcasted_iota(jnp.int32, (c, c), 1)
    bd_c = (lax.shift_right_logical(lax.broadcasted_iota(jnp.int32, (hc, hc), 0), LEVELS)
            == lax.shift_right_logical(lax.broadcasted_iota(jnp.int32, (hc, hc), 1), LEVELS))
    row_g = lax.broadcasted_iota(jnp.int32, (gw, gw), 0)
    col_g = lax.broadcasted_iota(jnp.int32, (gw, gw), 1)
    bd_g = lax.shift_right_logical(row_g, HEAD_SHIFT) == lax.shift_right_logical(col_g, HEAD_SHIFT)
    eye_g = row_g == col_g
    sel = sel_ref[...]
    tri = tri_ref[...]
    dist_row = lax.broadcasted_iota(jnp.int32, (c, hc), 0) ^ (lax.broadcasted_iota(jnp.int32, (c, hc), 1) & (c - 1))
    dist_bd = ((lax.broadcasted_iota(jnp.int32, (hc, hc), 0) ^ lax.broadcasted_iota(jnp.int32, (hc, hc), 1))
               & (c - 1))
    inv_masks = {"base_row": dist_row < INVERSE_BASE, "base_bd": bd_c & (dist_bd < INVERSE_BASE), "merge_bd": []}
    m = INVERSE_BASE
    while m < c:
        inv_masks["merge_bd"].append(bd_c & (dist_bd >= m) & (dist_bd < 2 * m))
        m *= 2

    _interleave([features(0)])
    p_rw_st, st_h, st_g = rw_p[...], hg_st[...], gl_st[...]
    for s_i in range(n_sub):
        gens = []
        rates = []
        for u in range(CHUNK_UNROLL):
            r0 = s_i * sub + u * c
            rates += [1, GLA_STAGE_PERIOD, GLA_STAGE_PERIOD]

            def ft(i, r0=r0):
                return feat[r0:r0 + c, i * gw:(i + 1) * gw]

            gens += [
                _rwkv_chunk(ft(_F_RR), ft(_F_RK), ft(_F_RV), ft(_F_RA), ft(_F_RB), ft(_F_RW),
                            tri, hms, bd_c, bd_g, eye_g, inv_masks),
                _gla_chunk(ft(_F_HQ), ft(_F_HK), ft(_F_HV), ft(_F_HGD), sel, hms, xr, bd_g),
                _gla_chunk(ft(_F_GQ), ft(_F_GK), ft(_F_GV), ft(_F_GGD), sel, hms, xr, bd_g)]
        if s_i + 1 < n_sub:
            gens.append(features((s_i + 1) * sub))
            rates.append(1)
        if s_i >= 1:
            gens.append(outputs((s_i - 1) * sub))
            rates.append(1)
        pre = _interleave(gens, rates)
        for u in range(CHUNK_UNROLL):
            r0 = s_i * sub + u * c
            o_r, p_rw_st = _rwkv_state_step(pre[3 * u], p_rw_st)
            o_h, st_h = _gla_state_step(pre[3 * u + 1], st_h)
            o_g, st_g = _gla_state_step(pre[3 * u + 2], st_g)
            osc[r0:r0 + c, 0:gw] = o_h
            osc[r0:r0 + c, gw:2 * gw] = o_r
            osc[r0:r0 + c, 2 * gw:3 * gw] = o_g
    rw_p[...] = p_rw_st
    hg_st[...] = st_h
    gl_st[...] = st_g
    _interleave([outputs((n_sub - 1) * sub)])


def _selection_matrix():
    c = CHUNK
    m = np.zeros((2 * c, c), np.float32)
    for t in range(c):
        mid = (t // 4) * 4 + 1
        if t % 4 >= 2:
            m[t, mid + 1:t + 1] = 1.0
        else:
            m[t, t + 1:mid + 1] = 1.0
        m[c + t, :t + 1] = 1.0
    return m


def _const_spec(shape):
    nd = len(shape)
    return pl.BlockSpec(shape, lambda *_: (0,) * nd, pipeline_mode=pl.Buffered(1))


def _layer_spec(shape, layer):
    return pl.BlockSpec((None,) + tuple(shape[1:]), lambda *_: (layer, 0, 0), pipeline_mode=pl.Buffered(1))


def _mixer_call(p, consts, layer):
    b, t, _ = p.shape
    tb = TIME_TILE
    gw = GROUP_W
    in_specs = [pl.BlockSpec((1, tb, D_IN_PAD), lambda bi, ti: (bi, ti, 0))]
    in_specs += [_const_spec(a.shape) for a in consts]
    return pl.pallas_call(
        functools.partial(_mixer_kernel, layer=layer),
        grid=(b, t // tb),
        in_specs=in_specs,
        out_specs=pl.BlockSpec((1, tb, D_MODEL), lambda bi, ti: (bi, ti, 0)),
        out_shape=jax.ShapeDtypeStruct((b, t, D_MODEL), _MXU_DTYPE),
        scratch_shapes=[
            pltpu.VMEM((tb, _N_FEAT * gw), _F32),
            pltpu.VMEM((tb, 3 * gw), _F32),
            pltpu.VMEM((tb, 2 * gw), _F32),
            pltpu.VMEM((MAX_WINDOW, gw), _F32),
            pltpu.VMEM((1, 4 * gw), _F32),
            pltpu.VMEM((gw, gw), _F32),
            pltpu.VMEM((gw, gw), _F32),
            pltpu.VMEM((gw, gw), _F32),
        ],
        compiler_params=pltpu.CompilerParams(
            dimension_semantics=("arbitrary", "arbitrary"), vmem_limit_bytes=VMEM_LIMIT_BYTES),
        name=f"mixer_l{layer}",
    )(p, *consts)


def _ffn_kernel(*refs, final, fused_proj):
    if fused_proj:
        x_ref, y_ref, wp_ref, g_ref, win_ref, wout_ref, gf_ref, o_ref = refs
        x = x_ref[...] + jnp.dot(y_ref[...], wp_ref[...], preferred_element_type=_F32)
    else:
        x_ref, g_ref, win_ref, wout_ref, gf_ref, o_ref = refs
        x = x_ref[...]
    xg, rs = _rms_norm_factors(x, g_ref[...])
    acc = jnp.zeros(x.shape, _F32)
    lo = 0
    for width in FF_BLOCKS:
        gate = jnp.dot(xg, win_ref[:, lo:lo + width], preferred_element_type=_F32) * rs
        up = jnp.dot(xg, win_ref[:, D_FF + lo:D_FF + lo + width], preferred_element_type=_F32) * rs
        act = _mx(gate * _sigmoid(gate) * up)
        acc = acc + jnp.dot(act, wout_ref[lo:lo + width, :], preferred_element_type=_F32)
        lo += width
    y = x + 0.5 * acc
    if final:
        y = _rms_norm(y, gf_ref[...])
    o_ref[...] = y


def _ffn_call(x2, g, w_in, w_out, g_final, final, layer, proj=None):
    n, d = x2.shape
    tm = ROW_TILE
    row_spec = pl.BlockSpec((tm, d), lambda i: (i, 0))
    args = [x2]
    in_specs = [row_spec]
    if proj is not None:
        y2, w_proj = proj
        args += [y2, w_proj]
        in_specs += [pl.BlockSpec((tm, y2.shape[1]), lambda i: (i, 0)), _layer_spec(w_proj.shape, layer)]
    args += [g, w_in, w_out, g_final]
    in_specs += [_const_spec(g.shape), _layer_spec(w_in.shape, layer), _layer_spec(w_out.shape, layer),
                 _const_spec(g_final.shape)]
    return pl.pallas_call(
        functools.partial(_ffn_kernel, final=final, fused_proj=proj is not None),
        grid=(n // tm,),
        in_specs=in_specs,
        out_specs=row_spec,
        out_shape=jax.ShapeDtypeStruct((n, d), _F32),
        compiler_params=pltpu.CompilerParams(
            dimension_semantics=("arbitrary",), vmem_limit_bytes=VMEM_LIMIT_BYTES),
        name="ffn_proj" if proj is not None else "ffn",
    )(*args)


def _inproj_kernel(x_ref, g_ref, w_ref, o_ref):
    xg, rs = _rms_norm_factors(x_ref[...], g_ref[...])
    o_ref[...] = jnp.dot(xg, w_ref[...], preferred_element_type=_F32) * rs


def _inproj_call(x2, g, w, layer):
    n, d = x2.shape
    tm = ROW_TILE
    return pl.pallas_call(
        _inproj_kernel,
        grid=(n // tm,),
        in_specs=[pl.BlockSpec((tm, d), lambda i: (i, 0)), _const_spec(g.shape), _layer_spec(w.shape, layer)],
        out_specs=pl.BlockSpec((tm, w.shape[2]), lambda i: (i, 0)),
        out_shape=jax.ShapeDtypeStruct((n, w.shape[2]), _F32),
        compiler_params=pltpu.CompilerParams(
            dimension_semantics=("arbitrary",), vmem_limit_bytes=VMEM_LIMIT_BYTES),
        name="inproj",
    )(x2, g, w)


def _pad_rows(w, rows):
    return jnp.pad(w, ((0, rows - w.shape[0]), (0, 0)))


def kernel(x, norm_ffn1, ffn1_w_in, ffn1_w_out, norm_mix, w_in, w_out, pool_w, pool_b, pool_scale,
           hgrn_lb_logits, hgrn_norm, rwkv_mu, rwkv_w0, rwkv_w2, rwkv_a0, rwkv_a2, rwkv_g2, rwkv_k_k,
           rwkv_k_a, rwkv_r_k, rwkv_ln_w, rwkv_ln_b, gla_w2, gla_b, gla_norm, norm_ffn2, ffn2_w_in,
           ffn2_w_out, norm_final):
    b, t, d = x.shape
    depth = norm_ffn1.shape[0]
    gw = GROUP_W
    assert d == D_MODEL and t % TIME_TILE == 0 and (b * t) % ROW_TILE == 0
    assert TIME_TILE % (CHUNK * CHUNK_UNROLL) == 0
    assert sum(FF_BLOCKS) == D_FF and all(w % MXU_TILE == 0 for w in FF_BLOCKS)

    sel = jnp.asarray(_selection_matrix(), _MXU_DTYPE)
    tri = jnp.asarray(np.tril(np.ones((CHUNK, CHUNK), np.float32)), _MXU_DTYPE)
    head = np.arange(gw) // HEAD_DIM
    seg = jnp.asarray((head[:, None] == head[None, :]).astype(np.float32) / HEAD_DIM, _MXU_DTYPE)
    bdmask = jnp.asarray((head[:, None] == head[None, :]).astype(np.float32))
    gf = norm_final.reshape(1, d)

    w1_in, w1_out, w2_in, w2_out, w_mix_out = (_mx(w) for w in (ffn1_w_in, ffn1_w_out, ffn2_w_in, ffn2_w_out, w_out))
    w_mix_in = _mx(jnp.pad(w_in, ((0, 0), (0, 0), (0, D_IN_PAD - D_IN))))

    h = x.reshape(b * t, d)
    for l in range(depth):
        h = _ffn_call(h, norm_ffn1[l].reshape(1, d), w1_in, w1_out, gf, False, l)

        p = _inproj_call(h, norm_mix[l].reshape(1, d), w_mix_in, l)
        poolw = _mx(jnp.tile(pool_w[l].reshape(gw, POOL_CH), (1, len(POOL_WINDOWS))) * bdmask)
        w2p = _mx(_pad_rows(rwkv_w2[l], LANES))
        a2p = _mx(jnp.pad(rwkv_a2[l], ((RWKV_DECAY_LORA, 0), (0, 0))))
        vecs = jnp.stack([pool_b[l], pool_scale[l], hgrn_norm[l], rwkv_w0[l], rwkv_a0[l], rwkv_k_k[l],
                          rwkv_k_a[l], rwkv_r_k[l], rwkv_ln_w[l], rwkv_ln_b[l], gla_b[l], gla_norm[l]])
        consts = (sel, tri, seg, poolw, w2p, a2p, _mx(rwkv_g2[l]), _mx(_pad_rows(gla_w2[l], LANES)),
                  rwkv_mu[l].reshape(1, 4 * gw), _pad_rows(vecs, _N_VEC), hgrn_lb_logits)
        y = _mixer_call(p.reshape(b, t, D_IN_PAD), consts, l)
        h = _ffn_call(h, norm_ffn2[l].reshape(1, d), w2_in, w2_out, gf, l == depth - 1, l,
                      proj=(y.reshape(b * t, d), w_mix_out))
    return h.reshape(b, t, d)
```

```python
import functools

import numpy as np
import jax
import jax.numpy as jnp
from jax import lax
from jax.experimental import pallas as pl
from jax.experimental.pallas import tpu as pltpu

D_MODEL = 1024
GROUP_W = 256
HEAD_DIM = 64
N_HEADS = GROUP_W // HEAD_DIM
HEAD_SHIFT = HEAD_DIM.bit_length() - 1
QK_SCALE = HEAD_DIM ** -0.5
POOL_WINDOWS = (2, 4, 8, 16)
POOL_CH = GROUP_W // len(POOL_WINDOWS)
MAX_WINDOW = 16
RWKV_DECAY_LORA = 64
RWKV_A_LORA = 64
RWKV_GATE_LORA = 128
RWKV_GN_EPS = 64e-5
GLA_GATE_LORA = 16
GLA_GATE_TAU = 16.0
D_IN = 13 * GROUP_W + GLA_GATE_LORA
D_FF = 2816
NORM_EPS = 1e-6
GATE_FLOOR = 1e-30

LANES = 128
SUBLANES = 8
VMEM_LIMIT_BYTES = 56 * 1024 * 1024

D_IN_PAD = 13 * GROUP_W + LANES
CHUNK = 64
LEVELS = CHUNK.bit_length() - 1
TIME_TILE = 512
CHUNK_UNROLL = 2
SUB_ROWS = CHUNK * CHUNK_UNROLL
GROUP_ROUNDS = 10
INVERSE_BASE = 8
ROW_TILE = 512
MXU_TILE = 256
FF_BLOCKS = (1536, 1280)

_P_POOL = 0
_P_HG = GROUP_W
_P_RW = 5 * GROUP_W
_P_GLA = 9 * GROUP_W

_F_HQ, _F_HK, _F_HV, _F_HGD = 0, 1, 2, 3
_F_RR, _F_RK, _F_RV, _F_RA, _F_RB, _F_RW = 4, 5, 6, 7, 8, 9
_F_GQ, _F_GK, _F_GV, _F_GGD = 10, 11, 12, 13
_N_FEAT = 14

(_V_POOL_B, _V_POOL_SCALE, _V_HGRN_NORM, _V_W0, _V_A0, _V_KK, _V_KA, _V_RK, _V_LNW, _V_LNB,
 _V_GLA_B, _V_GLA_NORM) = range(12)
_N_VEC = 16

_MXU_DTYPE = jnp.bfloat16
_F32 = jnp.float32


def _mx(a):
    return a.astype(_MXU_DTYPE)


def _dot(a, b):
    return jnp.dot(_mx(a), _mx(b), preferred_element_type=_F32)


def _dot_nt(a, b):
    return lax.dot_general(_mx(a), _mx(b), (((1,), (1,)), ((), ())), preferred_element_type=_F32)


def _dot_tn(a, b):
    return lax.dot_general(_mx(a), _mx(b), (((0,), (0,)), ((), ())), preferred_element_type=_F32)


def _split(x, parts):
    out = []
    r = x
    for _ in range(parts - 1):
        h = r.astype(_MXU_DTYPE)
        out.append(h)
        r = r - h.astype(_F32)
    out.append(r.astype(_MXU_DTYPE))
    return out


def _dot_exact_lhs(m, x, parts=3):
    m = _mx(m)
    acc = None
    for piece in _split(x, parts):
        t = jnp.dot(m, piece, preferred_element_type=_F32)
        acc = t if acc is None else acc + t
    return acc


def _rms_norm(x, g, eps=NORM_EPS):
    return x * lax.rsqrt(jnp.mean(x * x, axis=-1, keepdims=True) + eps) * g


def _rms_norm_factors(x, g, eps=NORM_EPS):
    return _mx(x * g), lax.rsqrt(jnp.mean(x * x, axis=-1, keepdims=True) + eps)


def _sigmoid(x):
    return 0.5 * jnp.tanh(0.5 * x) + 0.5


def _softplus(x):
    return jnp.maximum(x, 0.0) + jnp.log(1.0 + jnp.exp(-jnp.abs(x)))


def _head_masks():
    lane = lax.broadcasted_iota(jnp.int32, (1, GROUP_W), 1)
    return [lax.shift_right_logical(lane, HEAD_SHIFT) == h for h in range(N_HEADS)]


def _stack_heads(x, hms):
    x = _mx(x)
    zero = jnp.zeros((), x.dtype)
    return jnp.concatenate([jnp.where(hm, x, zero) for hm in hms], axis=0)


def _mid_rows(b, n):
    c, gw = b.shape
    if n == c:
        return jnp.broadcast_to(b[n // 2 - 1:n // 2, :], b.shape)
    b3 = b.reshape(c // n, n, gw)
    return jnp.broadcast_to(b3[:, n // 2 - 1:n // 2, :], b3.shape).reshape(c, gw)


def _gla_chunk(q, k, v, g, sel, hms, xr, bd):
    c = CHUNK
    m = _dot_exact_lhs(sel, g)
    yield
    e2 = m[0:c]
    b = m[c:2 * c]
    f_b = jnp.exp(b)
    kv = jnp.where(bd, _dot_tn(v, k * jnp.exp(b[c - 1:c, :] - b)), 0.0)
    yield
    t_idx = lax.broadcasted_iota(jnp.int32, (c, GROUP_W), 0)
    n_piece = c // SUBLANES

    def pieces_of(x, rows_per_head):
        return [[x[h * rows_per_head + i * SUBLANES:h * rows_per_head + (i + 1) * SUBLANES]
                 for i in range(rows_per_head // SUBLANES)] for h in range(N_HEADS)]

    xr_p = [xr[i * SUBLANES:(i + 1) * SUBLANES] for i in range(n_piece)]
    s_0 = pieces_of(_dot_nt(_stack_heads(q, hms), k), c)
    diag = [xr_i < 1 for xr_i in xr_p]
    sc = [[jnp.where(diag[i], s_0[h][i], 0.0) for i in range(n_piece)] for h in range(N_HEADS)]
    yield
    for j in range(1, LEVELS + 1):
        n = 1 << j
        upper = (t_idx & (n - 1)) >= n // 2
        if j == 1:
            ex = jnp.where(upper, g, 0.0)
        elif j == 2:
            ex = e2
        else:
            bm = _mid_rows(b, n)
            ex = jnp.where(upper, b - bm, bm - b)
        f_j = jnp.exp(ex)
        kt = jnp.where(upper, 0.0, k * f_j)
        if n // 2 >= SUBLANES:
            up_idx = [i for i in range(n_piece) if (i * SUBLANES) % n >= n // 2]
            qt = jnp.concatenate([(q * f_j)[i * SUBLANES:(i + 1) * SUBLANES] for i in up_idx], axis=0)
        else:
            up_idx = list(range(n_piece))
            qt = jnp.where(upper, q * f_j, 0.0)
        s_j = pieces_of(_dot_nt(_stack_heads(qt, hms), kt), len(up_idx) * SUBLANES)
        yield
        same_block = {i: xr_p[i] < n for i in up_idx}
        for h in range(N_HEADS):
            for pos, i in enumerate(up_idx):
                add = s_j[h][pos] if j == LEVELS else jnp.where(same_block[i], s_j[h][pos], 0.0)
                sc[h][i] = sc[h][i] + add
    sc_row = jnp.concatenate([jnp.concatenate([sc[h][i] for h in range(N_HEADS)], axis=1)
                              for i in range(n_piece)], axis=0)
    o = _dot(sc_row, _stack_heads(v, hms))
    yield
    yield o, q * f_b, f_b[c - 1:c, :], kv


def _gla_state_step(pre, st):
    o_intra, qb, e_last, kv = pre
    return o_intra + _dot_nt(qb, st), st * e_last + kv


def _rwkv_chunk(r, k, v, al, be, lw, tri, hms, bd_c, bd_g, eye_g, inv_masks):
    c = CHUNK
    hc = N_HEADS * c
    gw = GROUP_W

    def bdiag(x_row):
        return jnp.where(bd_c, jnp.concatenate([_mx(x_row)] * N_HEADS, axis=0), jnp.zeros((), _MXU_DTYPE))

    cw = _dot_exact_lhs(tri, lw)
    yield
    cw_last = cw[c - 1:c, :]
    e_neg = jnp.exp(-cw)
    e_rem = jnp.exp(cw_last - cw)
    a_t = al * jnp.exp(cw - lw)
    r_t = r * jnp.exp(cw)
    prod = _dot_nt(jnp.concatenate([a_t, r_t], axis=0),
                   jnp.concatenate([_stack_heads(be * e_neg, hms), _stack_heads(k * e_neg, hms)], axis=0))
    yield
    t_r = lax.broadcasted_iota(jnp.int32, (c, hc), 0)
    s_r = lax.broadcasted_iota(jnp.int32, (c, hc), 1) & (c - 1)
    l_row = jnp.where(s_r < t_r, prod[0:c, 0:hc], 0.0)
    ak_row = jnp.where(s_r < t_r, prod[0:c, hc:2 * hc], 0.0)
    rb_row = jnp.where(s_r <= t_r, prod[c:2 * c, 0:hc], 0.0)
    rk_row = jnp.where(s_r <= t_r, prod[c:2 * c, hc:2 * hc], 0.0)

    l_tiled = jnp.concatenate([_mx(l_row)] * N_HEADS, axis=0)
    zero = jnp.zeros((), _MXU_DTYPE)
    t_row = jnp.where(s_r == t_r, 1.0, 0.0) - jnp.where(inv_masks["base_row"], l_row, 0.0)
    l_b = jnp.where(inv_masks["base_bd"], l_tiled, zero)
    p_row = _dot(jnp.where(inv_masks["base_row"], l_row, 0.0), l_b)
    yield
    power = 2
    while True:
        last = 2 * power >= INVERSE_BASE
        res = _dot(t_row if last else jnp.concatenate([p_row, t_row], axis=0), bdiag(p_row))
        yield
        if last:
            t_row = t_row + res
            break
        p_row = res[0:c]
        t_row = t_row + res[c:2 * c]
        power *= 2
    for off_bd in inv_masks["merge_bd"]:
        x = _dot(t_row, jnp.where(off_bd, l_tiled, zero))
        yield
        t_row = t_row - _dot(x, bdiag(t_row))
        yield

    v_st = _stack_heads(v, hms)
    mv = _dot(ak_row, v_st)
    yield
    tm = _dot(t_row, jnp.concatenate([_stack_heads(a_t, hms), _stack_heads(mv, hms)], axis=1))
    yield
    ta = tm[:, 0:gw]
    uv = tm[:, gw:2 * gw]
    qe = r_t - _dot(rb_row, _stack_heads(ta, hms))
    yield
    ol = _dot(jnp.concatenate([rk_row, -rb_row], axis=1),
              jnp.concatenate([v_st, _stack_heads(uv, hms)], axis=0))
    yield
    b_r = be * e_rem
    gm = jnp.where(eye_g, jnp.exp(cw_last), 0.0) - jnp.where(bd_g, _dot_tn(b_r, ta), 0.0)
    yield
    hm = jnp.where(bd_g, _dot_tn(jnp.concatenate([k * e_rem, -b_r], axis=0),
                                 jnp.concatenate([v, uv], axis=0)), 0.0)
    yield
    yield gm, hm, qe, ol


def _rwkv_state_step(pre, p):
    gm, hm, qe, ol = pre
    gw = GROUP_W
    res = _dot(jnp.concatenate([gm, qe], axis=0), p)
    return res[gw:] + ol, res[0:gw] + hm


def _run_tasks(tasks):
    live = list(range(len(tasks)))
    rnd = 0
    while live:
        for i in list(live):
            first, gen = tasks[i]
            if rnd < first:
                continue
            try:
                next(gen)
            except StopIteration:
                live.remove(i)
        rnd += 1


def _mixer_kernel(p_ref, sel_ref, tri_ref, seg_ref, poolw_ref, w2_ref, a2_ref, g2_ref, glaw2_ref,
                  mu_ref, vec_ref, lb_ref, y_ref,
                  feat, osc, aux, pool_prev, rw_prev, hg_st, rw_p, gl_st, *, layer):
    tb = TIME_TILE
    gw = GROUP_W
    sub = SUB_ROWS
    n_sub = tb // sub
    c = CHUNK
    hc = N_HEADS * c
    ti = pl.program_id(1)

    @pl.when(ti == 0)
    def _():
        pool_prev[...] = jnp.zeros_like(pool_prev)
        rw_prev[...] = jnp.zeros_like(rw_prev)
        hg_st[...] = jnp.zeros_like(hg_st)
        rw_p[...] = jnp.zeros_like(rw_p)
        gl_st[...] = jnp.zeros_like(gl_st)

    def vec(i):
        return vec_ref[i:i + 1, :]

    seg = seg_ref[...]
    hms = _head_masks()

    def features(r0):
        def pcol(base, i, n=1):
            return p_ref[0, r0:r0 + sub, base + i * gw: base + (i + n) * gw]

        def put(i, val):
            feat[r0:r0 + sub, i * gw:(i + 1) * gw] = val

        p_pool = pcol(_P_POOL, 0)
        before = pool_prev[...] if r0 == 0 else p_ref[0, r0 - MAX_WINDOW:r0, _P_POOL:_P_POOL + gw]
        ext = jnp.concatenate([before, p_pool], axis=0)
        if r0 + sub == tb:
            pool_prev[...] = p_pool[sub - MAX_WINDOW:, :]
        lane = lax.broadcasted_iota(jnp.int32, (sub, gw), 1)
        grp = lax.shift_right_logical(lane, HEAD_SHIFT)
        tpos = lax.broadcasted_iota(jnp.int32, (sub, gw), 0) + (ti * tb + r0)
        win = ext
        win_sum = jnp.zeros((sub, gw), _F32)
        count = jnp.zeros((sub, gw), _F32)
        for gi, w in enumerate(POOL_WINDOWS):
            sh = w // 2
            while sh < w:
                win = win + pltpu.roll(win, sh, 0)
                sh *= 2
            win_sum = jnp.where(grp == gi, win[MAX_WINDOW:, :], win_sum)
            count = jnp.where(grp == gi, jnp.minimum(tpos + 1, w).astype(_F32), count)
        pooled = win_sum / count - p_pool
        y_pool = (_dot(pooled, poolw_ref[...]) + vec(_V_POOL_B)) * vec(_V_POOL_SCALE)
        y_ref[0, r0:r0 + sub, 0:gw] = y_pool.astype(y_ref.dtype)
        yield

        sm = jnp.exp(lb_ref[...] - jnp.max(lb_ref[...], axis=0, keepdims=True))
        sm = sm / jnp.sum(sm, axis=0, keepdims=True)
        lb = jnp.sum(sm[0:layer + 1], axis=0, keepdims=True) - sm[0:1]
        hq = pcol(_P_HG, 0)
        f = lb + (1.0 - lb) * _sigmoid(pcol(_P_HG, 1))
        put(_F_HQ, hq * _sigmoid(hq) * QK_SCALE)
        put(_F_HK, 1.0 - f)
        put(_F_HV, pcol(_P_HG, 2))
        put(_F_HGD, jnp.log(jnp.maximum(f, GATE_FLOOR)))
        yield

        p_rw = pcol(_P_RW, 0, 4)
        before = rw_prev[...] if r0 == 0 else p_ref[0, r0 - 1:r0, _P_RW:_P_RW + 4 * gw]
        row0 = lax.broadcasted_iota(jnp.int32, p_rw.shape, 0) == 0
        prev = jnp.where(row0, before, pltpu.roll(p_rw, 1, 0))
        if r0 + sub == tb:
            rw_prev[...] = p_rw[sub - 1:sub, :]
        p_rw = p_rw + (prev - p_rw) * mu_ref[...]
        rr = p_rw[:, 0:gw]
        rk = p_rw[:, gw:2 * gw]
        rv = p_rw[:, 2 * gw:3 * gw]
        xwa = p_rw[:, 3 * gw:3 * gw + LANES]
        xg = p_rw[:, 3 * gw + LANES:4 * gw]
        w_log = -_softplus(-(vec(_V_W0) + _dot(jnp.tanh(xwa), w2_ref[...]))) - 0.5
        yield
        a = _sigmoid(vec(_V_A0) + _dot(xwa, a2_ref[...]))
        yield
        aux[r0:r0 + sub, gw:2 * gw] = _dot(_sigmoid(xg), g2_ref[...])
        yield
        kk = rk * vec(_V_KK)
        kk = kk / jnp.maximum(jnp.sqrt(_dot(kk * kk, seg) * HEAD_DIM), 1e-12)
        yield
        rk = rk * (1.0 + (a - 1.0) * vec(_V_KA))
        put(_F_RR, rr)
        put(_F_RK, rk)
        put(_F_RV, rv)
        put(_F_RA, kk)
        put(_F_RB, kk * a)
        put(_F_RW, -jnp.exp(w_log))
        aux[r0:r0 + sub, 0:gw] = _dot(rr * rk * vec(_V_RK), seg) * HEAD_DIM * rv
        yield

        ga = p_ref[0, r0:r0 + sub, _P_GLA + 4 * gw:_P_GLA + 4 * gw + LANES]
        z = _dot(ga, glaw2_ref[...]) + vec(_V_GLA_B)
        put(_F_GQ, pcol(_P_GLA, 0) * QK_SCALE)
        put(_F_GK, pcol(_P_GLA, 1))
        put(_F_GV, pcol(_P_GLA, 2))
        put(_F_GGD, -_softplus(-z) / GLA_GATE_TAU)
        yield
        yield True

    def outputs(r0):
        def pcol(base, i):
            return p_ref[0, r0:r0 + sub, base + i * gw: base + (i + 1) * gw]

        o_h = osc[r0:r0 + sub, 0:gw]
        y_hg = _rms_norm(o_h, vec(_V_HGRN_NORM)) * _sigmoid(pcol(_P_HG, 3))
        y_ref[0, r0:r0 + sub, gw:2 * gw] = y_hg.astype(y_ref.dtype)
        yield
        o_r = osc[r0:r0 + sub, gw:2 * gw]
        cen = o_r - _dot(o_r, seg)
        yield
        var = _dot(cen * cen, seg)
        gn = cen * lax.rsqrt(var + RWKV_GN_EPS) * vec(_V_LNW) + vec(_V_LNB)
        y_rw = (gn + aux[r0:r0 + sub, 0:gw]) * aux[r0:r0 + sub, gw:2 * gw]
        y_ref[0, r0:r0 + sub, 2 * gw:3 * gw] = y_rw.astype(y_ref.dtype)
        yield
        o_g = osc[r0:r0 + sub, 2 * gw:3 * gw]
        ms = _dot(o_g * o_g, seg)
        gg = pcol(_P_GLA, 3)
        y_gl = o_g * lax.rsqrt(ms + NORM_EPS) * vec(_V_GLA_NORM) * (gg * _sigmoid(gg))
        y_ref[0, r0:r0 + sub, 3 * gw:4 * gw] = y_gl.astype(y_ref.dtype)
        yield
        yield True

    xr = lax.broadcasted_iota(jnp.int32, (c, c), 0) ^ lax.broadcasted_iota(jnp.int32, (c, c), 1)
    bd_c = (lax.shift_right_logical(lax.broadcasted_iota(jnp.int32, (hc, hc), 0), LEVELS)
            == lax.shift_right_logical(lax.broadcasted_iota(jnp.int32, (hc, hc), 1), LEVELS))
    row_g = lax.broadcasted_iota(jnp.int32, (gw, gw), 0)
    col_g = lax.broadcasted_iota(jnp.int32, (gw, gw), 1)
    bd_g = lax.shift_right_logical(row_g, HEAD_SHIFT) == lax.shift_right_logical(col_g, HEAD_SHIFT)
    eye_g = row_g == col_g
    sel = sel_ref[...]
    tri = tri_ref[...]
    dist_row = lax.broadcasted_iota(jnp.int32, (c, hc), 0) ^ (lax.broadcasted_iota(jnp.int32, (c, hc), 1) & (c - 1))
    dist_bd = ((lax.broadcasted_iota(jnp.int32, (hc, hc), 0) ^ lax.broadcasted_iota(jnp.int32, (hc, hc), 1))
               & (c - 1))
    inv_masks = {"base_row": dist_row < INVERSE_BASE, "base_bd": bd_c & (dist_bd < INVERSE_BASE), "merge_bd": []}
    m = INVERSE_BASE
    while m < c:
        inv_masks["merge_bd"].append(bd_c & (dist_bd >= m) & (dist_bd < 2 * m))
        m *= 2

    state = [rw_p[...], hg_st[...], gl_st[...]]
    feat_ready, pre_done, state_done = set(), {}, set()

    def features_task(s_i):
        for _ in features(s_i * sub):
            yield
        feat_ready.add(s_i)

    def chunk_task(s_i, u, kind):
        while s_i not in feat_ready:
            yield
        r0 = s_i * sub + u * c

        def ft(i):
            return feat[r0:r0 + c, i * gw:(i + 1) * gw]

        if kind == 0:
            gen = _rwkv_chunk(ft(_F_RR), ft(_F_RK), ft(_F_RV), ft(_F_RA), ft(_F_RB), ft(_F_RW),
                              tri, hms, bd_c, bd_g, eye_g, inv_masks)
        elif kind == 1:
            gen = _gla_chunk(ft(_F_HQ), ft(_F_HK), ft(_F_HV), ft(_F_HGD), sel, hms, xr, bd_g)
        else:
            gen = _gla_chunk(ft(_F_GQ), ft(_F_GK), ft(_F_GV), ft(_F_GGD), sel, hms, xr, bd_g)
        for out in gen:
            if out is None:
                yield
            else:
                pre_done[(s_i, u, kind)] = out

    def state_task(s_i):
        keys = [(s_i, u, kind) for u in range(CHUNK_UNROLL) for kind in range(3)]
        while (s_i > 0 and (s_i - 1) not in state_done) or any(k not in pre_done for k in keys):
            yield
        for u in range(CHUNK_UNROLL):
            r0 = s_i * sub + u * c
            o_r, state[0] = _rwkv_state_step(pre_done[(s_i, u, 0)], state[0])
            yield
            o_h, state[1] = _gla_state_step(pre_done[(s_i, u, 1)], state[1])
            yield
            o_g, state[2] = _gla_state_step(pre_done[(s_i, u, 2)], state[2])
            osc[r0:r0 + c, 0:gw] = o_h
            osc[r0:r0 + c, gw:2 * gw] = o_r
            osc[r0:r0 + c, 2 * gw:3 * gw] = o_g
            yield
        state_done.add(s_i)

    def outputs_task(s_i):
        while s_i not in state_done:
            yield
        for _ in outputs(s_i * sub):
            yield

    tasks = []
    for s_i in range(n_sub):
        tasks.append((max(0, s_i - 1) * GROUP_ROUNDS, features_task(s_i)))
        for u in range(CHUNK_UNROLL):
            for kind in range(3):
                tasks.append((s_i * GROUP_ROUNDS, chunk_task(s_i, u, kind)))
        tasks.append((s_i * GROUP_ROUNDS, state_task(s_i)))
        tasks.append((s_i * GROUP_ROUNDS, outputs_task(s_i)))
    _run_tasks(tasks)
    rw_p[...], hg_st[...], gl_st[...] = state


def _selection_matrix():
    c = CHUNK
    m = np.zeros((2 * c, c), np.float32)
    for t in range(c):
        mid = (t // 4) * 4 + 1
        if t % 4 >= 2:
            m[t, mid + 1:t + 1] = 1.0
        else:
            m[t, t + 1:mid + 1] = 1.0
        m[c + t, :t + 1] = 1.0
    return m


def _const_spec(shape):
    nd = len(shape)
    return pl.BlockSpec(shape, lambda *_: (0,) * nd, pipeline_mode=pl.Buffered(1))


def _layer_spec(shape, layer):
    return pl.BlockSpec((None,) + tuple(shape[1:]), lambda *_: (layer, 0, 0), pipeline_mode=pl.Buffered(1))


def _mixer_call(p, consts, layer):
    b, t, _ = p.shape
    tb = TIME_TILE
    gw = GROUP_W
    in_specs = [pl.BlockSpec((1, tb, D_IN_PAD), lambda bi, ti: (bi, ti, 0))]
    in_specs += [_const_spec(a.shape) for a in consts]
    return pl.pallas_call(
        functools.partial(_mixer_kernel, layer=layer),
        grid=(b, t // tb),
        in_specs=in_specs,
        out_specs=pl.BlockSpec((1, tb, D_MODEL), lambda bi, ti: (bi, ti, 0)),
        out_shape=jax.ShapeDtypeStruct((b, t, D_MODEL), _MXU_DTYPE),
        scratch_shapes=[
            pltpu.VMEM((tb, _N_FEAT * gw), _F32),
            pltpu.VMEM((tb, 3 * gw), _F32),
            pltpu.VMEM((tb, 2 * gw), _F32),
            pltpu.VMEM((MAX_WINDOW, gw), _F32),
            pltpu.VMEM((1, 4 * gw), _F32),
            pltpu.VMEM((gw, gw), _F32),
            pltpu.VMEM((gw, gw), _F32),
            pltpu.VMEM((gw, gw), _F32),
        ],
        compiler_params=pltpu.CompilerParams(
            dimension_semantics=("arbitrary", "arbitrary"), vmem_limit_bytes=VMEM_LIMIT_BYTES),
        name=f"mixer_l{layer}",
    )(p, *consts)


def _ffn_kernel(*refs, final, fused_proj):
    if fused_proj:
        x_ref, y_ref, wp_ref, g_ref, win_ref, wout_ref, gf_ref, o_ref = refs
        x = x_ref[...] + jnp.dot(y_ref[...], wp_ref[...], preferred_element_type=_F32)
    else:
        x_ref, g_ref, win_ref, wout_ref, gf_ref, o_ref = refs
        x = x_ref[...]
    xg, rs = _rms_norm_factors(x, g_ref[...])
    acc = jnp.zeros(x.shape, _F32)
    lo = 0
    for width in FF_BLOCKS:
        gate = jnp.dot(xg, win_ref[:, lo:lo + width], preferred_element_type=_F32) * rs
        up = jnp.dot(xg, win_ref[:, D_FF + lo:D_FF + lo + width], preferred_element_type=_F32) * rs
        act = _mx(gate * _sigmoid(gate) * up)
        acc = acc + jnp.dot(act, wout_ref[lo:lo + width, :], preferred_element_type=_F32)
        lo += width
    y = x + 0.5 * acc
    if final:
        y = _rms_norm(y, gf_ref[...])
    o_ref[...] = y


def _ffn_call(x2, g, w_in, w_out, g_final, final, layer, proj=None):
    n, d = x2.shape
    tm = ROW_TILE
    row_spec = pl.BlockSpec((tm, d), lambda i: (i, 0))
    args = [x2]
    in_specs = [row_spec]
    if proj is not None:
        y2, w_proj = proj
        args += [y2, w_proj]
        in_specs += [pl.BlockSpec((tm, y2.shape[1]), lambda i: (i, 0)), _layer_spec(w_proj.shape, layer)]
    args += [g, w_in, w_out, g_final]
    in_specs += [_const_spec(g.shape), _layer_spec(w_in.shape, layer), _layer_spec(w_out.shape, layer),
                 _const_spec(g_final.shape)]
    return pl.pallas_call(
        functools.partial(_ffn_kernel, final=final, fused_proj=proj is not None),
        grid=(n // tm,),
        in_specs=in_specs,
        out_specs=row_spec,
        out_shape=jax.ShapeDtypeStruct((n, d), _F32),
        compiler_params=pltpu.CompilerParams(
            dimension_semantics=("arbitrary",), vmem_limit_bytes=VMEM_LIMIT_BYTES),
        name="ffn_proj" if proj is not None else "ffn",
    )(*args)


def _inproj_kernel(x_ref, g_ref, w_ref, o_ref):
    xg, rs = _rms_norm_factors(x_ref[...], g_ref[...])
    o_ref[...] = jnp.dot(xg, w_ref[...], preferred_element_type=_F32) * rs


def _inproj_call(x2, g, w, layer):
    n, d = x2.shape
    tm = ROW_TILE
    return pl.pallas_call(
        _inproj_kernel,
        grid=(n // tm,),
        in_specs=[pl.BlockSpec((tm, d), lambda i: (i, 0)), _const_spec(g.shape), _layer_spec(w.shape, layer)],
        out_specs=pl.BlockSpec((tm, w.shape[2]), lambda i: (i, 0)),
        out_shape=jax.ShapeDtypeStruct((n, w.shape[2]), _F32),
        compiler_params=pltpu.CompilerParams(
            dimension_semantics=("arbitrary",), vmem_limit_bytes=VMEM_LIMIT_BYTES),
        name="inproj",
    )(x2, g, w)


def _pad_rows(w, rows):
    return jnp.pad(w, ((0, rows - w.shape[0]), (0, 0)))


def kernel(x, norm_ffn1, ffn1_w_in, ffn1_w_out, norm_mix, w_in, w_out, pool_w, pool_b, pool_scale,
           hgrn_lb_logits, hgrn_norm, rwkv_mu, rwkv_w0, rwkv_w2, rwkv_a0, rwkv_a2, rwkv_g2, rwkv_k_k,
           rwkv_k_a, rwkv_r_k, rwkv_ln_w, rwkv_ln_b, gla_w2, gla_b, gla_norm, norm_ffn2, ffn2_w_in,
           ffn2_w_out, norm_final):
    b, t, d = x.shape
    depth = norm_ffn1.shape[0]
    gw = GROUP_W
    assert d == D_MODEL and t % TIME_TILE == 0 and (b * t) % ROW_TILE == 0
    assert TIME_TILE % (CHUNK * CHUNK_UNROLL) == 0
    assert sum(FF_BLOCKS) == D_FF and all(w % MXU_TILE == 0 for w in FF_BLOCKS)

    sel = jnp.asarray(_selection_matrix(), _MXU_DTYPE)
    tri = jnp.asarray(np.tril(np.ones((CHUNK, CHUNK), np.float32)), _MXU_DTYPE)
    head = np.arange(gw) // HEAD_DIM
    seg = jnp.asarray((head[:, None] == head[None, :]).astype(np.float32) / HEAD_DIM, _MXU_DTYPE)
    bdmask = jnp.asarray((head[:, None] == head[None, :]).astype(np.float32))
    gf = norm_final.reshape(1, d)

    w1_in, w1_out, w2_in, w2_out, w_mix_out = (_mx(w) for w in (ffn1_w_in, ffn1_w_out, ffn2_w_in, ffn2_w_out, w_out))
    w_mix_in = _mx(jnp.pad(w_in, ((0, 0), (0, 0), (0, D_IN_PAD - D_IN))))

    h = x.reshape(b * t, d)
    for l in range(depth):
        h = _ffn_call(h, norm_ffn1[l].reshape(1, d), w1_in, w1_out, gf, False, l)

        p = _inproj_call(h, norm_mix[l].reshape(1, d), w_mix_in, l)
        poolw = _mx(jnp.tile(pool_w[l].reshape(gw, POOL_CH), (1, len(POOL_WINDOWS))) * bdmask)
        w2p = _mx(_pad_rows(rwkv_w2[l], LANES))
        a2p = _mx(jnp.pad(rwkv_a2[l], ((RWKV_DECAY_LORA, 0), (0, 0))))
        vecs = jnp.stack([pool_b[l], pool_scale[l], hgrn_norm[l], rwkv_w0[l], rwkv_a0[l], rwkv_k_k[l],
                          rwkv_k_a[l], rwkv_r_k[l], rwkv_ln_w[l], rwkv_ln_b[l], gla_b[l], gla_norm[l]])
        consts = (sel, tri, seg, poolw, w2p, a2p, _mx(rwkv_g2[l]), _mx(_pad_rows(gla_w2[l], LANES)),
                  rwkv_mu[l].reshape(1, 4 * gw), _pad_rows(vecs, _N_VEC), hgrn_lb_logits)
        y = _mixer_call(p.reshape(b, t, D_IN_PAD), consts, l)
        h = _ffn_call(h, norm_ffn2[l].reshape(1, d), w2_in, w2_out, gf, l == depth - 1, l,
                      proj=(y.reshape(b * t, d), w_mix_out))
    return h.reshape(b, t, d)
```

```python
import functools

import numpy as np
import jax
import jax.numpy as jnp
from jax import lax
from jax.experimental import pallas as pl
from jax.experimental.pallas import tpu as pltpu

D_MODEL = 1024
GROUP_W = 256
HEAD_DIM = 64
N_HEADS = GROUP_W // HEAD_DIM
HEAD_SHIFT = HEAD_DIM.bit_length() - 1
QK_SCALE = HEAD_DIM ** -0.5
POOL_WINDOWS = (2, 4, 8, 16)
POOL_CH = GROUP_W // len(POOL_WINDOWS)
MAX_WINDOW = 16
RWKV_DECAY_LORA = 64
RWKV_A_LORA = 64
RWKV_GATE_LORA = 128
RWKV_GN_EPS = 64e-5
GLA_GATE_LORA = 16
GLA_GATE_TAU = 16.0
D_IN = 13 * GROUP_W + GLA_GATE_LORA
D_FF = 2816
NORM_EPS = 1e-6
GATE_FLOOR = 1e-30

LANES = 128
SUBLANES = 8
VMEM_LIMIT_BYTES = 56 * 1024 * 1024

D_IN_PAD = 13 * GROUP_W + LANES
CHUNK = 64
LEVELS = CHUNK.bit_length() - 1
TIME_TILE = 512
CHUNK_UNROLL = 2
SUB_ROWS = CHUNK * CHUNK_UNROLL
GROUP_ROUNDS = 10
INVERSE_BASE = 8
ROW_TILE = 512
MXU_TILE = 256
FF_BLOCKS = (1536, 1280)

_P_POOL = 0
_P_HG = GROUP_W
_P_RW = 5 * GROUP_W
_P_GLA = 9 * GROUP_W

_F_HQ, _F_HK, _F_HV, _F_HGD = 0, 1, 2, 3
_F_RR, _F_RK, _F_RV, _F_RA, _F_RB, _F_RW = 4, 5, 6, 7, 8, 9
_F_GQ, _F_GK, _F_GV, _F_GGD = 10, 11, 12, 13
_N_FEAT = 14

(_V_POOL_B, _V_POOL_SCALE, _V_HGRN_NORM, _V_W0, _V_A0, _V_KK, _V_KA, _V_RK, _V_LNW, _V_LNB,
 _V_GLA_B, _V_GLA_NORM) = range(12)
_N_VEC = 16

_MXU_DTYPE = jnp.bfloat16
_F32 = jnp.float32


def _mx(a):
    return a.astype(_MXU_DTYPE)


def _dot(a, b):
    return jnp.dot(_mx(a), _mx(b), preferred_element_type=_F32)


def _dot_nt(a, b):
    return lax.dot_general(_mx(a), _mx(b), (((1,), (1,)), ((), ())), preferred_element_type=_F32)


def _dot_tn(a, b):
    return lax.dot_general(_mx(a), _mx(b), (((0,), (0,)), ((), ())), preferred_element_type=_F32)


def _split(x, parts):
    out = []
    r = x
    for _ in range(parts - 1):
        h = r.astype(_MXU_DTYPE)
        out.append(h)
        r = r - h.astype(_F32)
    out.append(r.astype(_MXU_DTYPE))
    return out


def _dot_exact_lhs(m, x, parts=3):
    m = _mx(m)
    acc = None
    for piece in _split(x, parts):
        t = jnp.dot(m, piece, preferred_element_type=_F32)
        acc = t if acc is None else acc + t
    return acc


def _rms_norm(x, g, eps=NORM_EPS):
    return x * lax.rsqrt(jnp.mean(x * x, axis=-1, keepdims=True) + eps) * g


def _rms_norm_factors(x, g, eps=NORM_EPS):
    return _mx(x * g), lax.rsqrt(jnp.mean(x * x, axis=-1, keepdims=True) + eps)


def _sigmoid(x):
    return 0.5 * jnp.tanh(0.5 * x) + 0.5


def _softplus(x):
    return jnp.maximum(x, 0.0) + jnp.log(1.0 + jnp.exp(-jnp.abs(x)))


def _head_masks():
    lane = lax.broadcasted_iota(jnp.int32, (1, GROUP_W), 1)
    return [lax.shift_right_logical(lane, HEAD_SHIFT) == h for h in range(N_HEADS)]


def _stack_heads(x, hms):
    x = _mx(x)
    zero = jnp.zeros((), x.dtype)
    return jnp.concatenate([jnp.where(hm, x, zero) for hm in hms], axis=0)


def _mid_rows(b, n):
    c, gw = b.shape
    if n == c:
        return jnp.broadcast_to(b[n // 2 - 1:n // 2, :], b.shape)
    b3 = b.reshape(c // n, n, gw)
    return jnp.broadcast_to(b3[:, n // 2 - 1:n // 2, :], b3.shape).reshape(c, gw)


def _gla_chunk(q, k, v, g, sel, seg, hms, xr, bd):
    c = CHUNK
    m = _dot_exact_lhs(sel, g)
    yield
    e2 = m[0:c]
    b = m[c:2 * c]
    f_b = jnp.exp(b)
    kv = jnp.where(bd, _dot_tn(v, k * jnp.exp(b[c - 1:c, :] - b)), 0.0)
    yield
    t_idx = lax.broadcasted_iota(jnp.int32, (c, GROUP_W), 0)
    n_piece = c // SUBLANES

    def pieces_of(x, rows_per_head):
        return [[x[h * rows_per_head + i * SUBLANES:h * rows_per_head + (i + 1) * SUBLANES]
                 for i in range(rows_per_head // SUBLANES)] for h in range(N_HEADS)]

    xr_p = [xr[i * SUBLANES:(i + 1) * SUBLANES] for i in range(n_piece)]
    odd = (t_idx & 1) == 1
    w_same = q * k
    w_prev = jnp.where(odd, q * jnp.exp(g) * pltpu.roll(k, 1, 0), 0.0)
    d = _dot(jnp.concatenate([w_same, w_prev], axis=0), seg) * HEAD_DIM
    o = d[0:c] * v + d[c:2 * c] * pltpu.roll(v, 1, 0)
    yield
    sc = None
    for j in range(2, LEVELS + 1):
        n = 1 << j
        upper = (t_idx & (n - 1)) >= n // 2
        if j == 2:
            ex = e2
        else:
            bm = _mid_rows(b, n)
            ex = jnp.where(upper, b - bm, bm - b)
        f_j = jnp.exp(ex)
        kt = jnp.where(upper, 0.0, k * f_j)
        if n // 2 >= SUBLANES:
            up_idx = [i for i in range(n_piece) if (i * SUBLANES) % n >= n // 2]
            qt = jnp.concatenate([(q * f_j)[i * SUBLANES:(i + 1) * SUBLANES] for i in up_idx], axis=0)
        else:
            up_idx = list(range(n_piece))
            qt = jnp.where(upper, q * f_j, 0.0)
        s_j = pieces_of(_dot_nt(_stack_heads(qt, hms), kt), len(up_idx) * SUBLANES)
        yield
        same_block = {i: xr_p[i] < n for i in up_idx}
        if sc is None:
            sc = [[jnp.where(same_block[i], s_j[h][i], 0.0) for i in range(n_piece)] for h in range(N_HEADS)]
            continue
        for h in range(N_HEADS):
            for pos, i in enumerate(up_idx):
                add = s_j[h][pos] if j == LEVELS else jnp.where(same_block[i], s_j[h][pos], 0.0)
                sc[h][i] = sc[h][i] + add
    sc_row = jnp.concatenate([jnp.concatenate([sc[h][i] for h in range(N_HEADS)], axis=1)
                              for i in range(n_piece)], axis=0)
    o = o + _dot(sc_row, _stack_heads(v, hms))
    yield
    yield o, q * f_b, f_b[c - 1:c, :], kv


def _gla_state_step(pre, st):
    o_intra, qb, e_last, kv = pre
    return o_intra + _dot_nt(qb, st), st * e_last + kv


def _rwkv_chunk(r, k, v, al, be, lw, tri, hms, bd_c, bd_g, eye_g, inv_masks):
    c = CHUNK
    hc = N_HEADS * c
    gw = GROUP_W

    def bdiag(x_row):
        return jnp.where(bd_c, jnp.concatenate([_mx(x_row)] * N_HEADS, axis=0), jnp.zeros((), _MXU_DTYPE))

    cw = _dot_exact_lhs(tri, lw)
    yield
    cw_last = cw[c - 1:c, :]
    e_neg = jnp.exp(-cw)
    e_rem = jnp.exp(cw_last - cw)
    a_t = al * jnp.exp(cw - lw)
    r_t = r * jnp.exp(cw)
    prod = _dot_nt(jnp.concatenate([a_t, r_t], axis=0),
                   jnp.concatenate([_stack_heads(be * e_neg, hms), _stack_heads(k * e_neg, hms)], axis=0))
    yield
    t_r = lax.broadcasted_iota(jnp.int32, (c, hc), 0)
    s_r = lax.broadcasted_iota(jnp.int32, (c, hc), 1) & (c - 1)
    l_row = jnp.where(s_r < t_r, prod[0:c, 0:hc], 0.0)
    ak_row = jnp.where(s_r < t_r, prod[0:c, hc:2 * hc], 0.0)
    rb_row = jnp.where(s_r <= t_r, prod[c:2 * c, 0:hc], 0.0)
    rk_row = jnp.where(s_r <= t_r, prod[c:2 * c, hc:2 * hc], 0.0)

    l_tiled = jnp.concatenate([_mx(l_row)] * N_HEADS, axis=0)
    zero = jnp.zeros((), _MXU_DTYPE)
    t_row = jnp.where(s_r == t_r, 1.0, 0.0) - jnp.where(inv_masks["base_row"], l_row, 0.0)
    l_b = jnp.where(inv_masks["base_bd"], l_tiled, zero)
    p_row = _dot(jnp.where(inv_masks["base_row"], l_row, 0.0), l_b)
    yield
    power = 2
    while True:
        last = 2 * power >= INVERSE_BASE
        res = _dot(t_row if last else jnp.concatenate([p_row, t_row], axis=0), bdiag(p_row))
        yield
        if last:
            t_row = t_row + res
            break
        p_row = res[0:c]
        t_row = t_row + res[c:2 * c]
        power *= 2
    for off_bd in inv_masks["merge_bd"]:
        x = _dot(t_row, jnp.where(off_bd, l_tiled, zero))
        yield
        t_row = t_row - _dot(x, bdiag(t_row))
        yield

    v_st = _stack_heads(v, hms)
    mv = _dot(ak_row, v_st)
    yield
    tm = _dot(t_row, jnp.concatenate([_stack_heads(a_t, hms), _stack_heads(mv, hms)], axis=1))
    yield
    ta = tm[:, 0:gw]
    uv = tm[:, gw:2 * gw]
    qe = r_t - _dot(rb_row, _stack_heads(ta, hms))
    yield
    ol = _dot(jnp.concatenate([rk_row, -rb_row], axis=1),
              jnp.concatenate([v_st, _stack_heads(uv, hms)], axis=0))
    yield
    b_r = be * e_rem
    gm = jnp.where(eye_g, jnp.exp(cw_last), 0.0) - jnp.where(bd_g, _dot_tn(b_r, ta), 0.0)
    yield
    hm = jnp.where(bd_g, _dot_tn(jnp.concatenate([k * e_rem, -b_r], axis=0),
                                 jnp.concatenate([v, uv], axis=0)), 0.0)
    yield
    yield gm, hm, qe, ol


def _rwkv_state_step(pre, p):
    gm, hm, qe, ol = pre
    gw = GROUP_W
    res = _dot(jnp.concatenate([gm, qe], axis=0), p)
    return res[gw:] + ol, res[0:gw] + hm


def _run_tasks(tasks):
    live = list(range(len(tasks)))
    rnd = 0
    while live:
        for i in list(live):
            first, gen = tasks[i]
            if rnd < first:
                continue
            try:
                next(gen)
            except StopIteration:
                live.remove(i)
        rnd += 1


def _mixer_kernel(p_ref, sel_ref, tri_ref, seg_ref, poolw_ref, w2_ref, a2_ref, g2_ref, glaw2_ref,
                  mu_ref, vec_ref, lb_ref, y_ref,
                  feat, osc, aux, pool_prev, rw_prev, hg_st, rw_p, gl_st, *, layer):
    tb = TIME_TILE
    gw = GROUP_W
    sub = SUB_ROWS
    n_sub = tb // sub
    c = CHUNK
    hc = N_HEADS * c
    ti = pl.program_id(1)

    @pl.when(ti == 0)
    def _():
        pool_prev[...] = jnp.zeros_like(pool_prev)
        rw_prev[...] = jnp.zeros_like(rw_prev)
        hg_st[...] = jnp.zeros_like(hg_st)
        rw_p[...] = jnp.zeros_like(rw_p)
        gl_st[...] = jnp.zeros_like(gl_st)

    def vec(i):
        return vec_ref[i:i + 1, :]

    seg = seg_ref[...]
    hms = _head_masks()

    def features(r0):
        def pcol(base, i, n=1):
            return p_ref[0, r0:r0 + sub, base + i * gw: base + (i + n) * gw]

        def put(i, val):
            feat[r0:r0 + sub, i * gw:(i + 1) * gw] = val

        p_pool = pcol(_P_POOL, 0)
        before = pool_prev[...] if r0 == 0 else p_ref[0, r0 - MAX_WINDOW:r0, _P_POOL:_P_POOL + gw]
        ext = jnp.concatenate([before, p_pool], axis=0)
        if r0 + sub == tb:
            pool_prev[...] = p_pool[sub - MAX_WINDOW:, :]
        lane = lax.broadcasted_iota(jnp.int32, (sub, gw), 1)
        grp = lax.shift_right_logical(lane, HEAD_SHIFT)
        tpos = lax.broadcasted_iota(jnp.int32, (sub, gw), 0) + (ti * tb + r0)
        win = ext
        win_sum = jnp.zeros((sub, gw), _F32)
        count = jnp.zeros((sub, gw), _F32)
        for gi, w in enumerate(POOL_WINDOWS):
            sh = w // 2
            while sh < w:
                win = win + pltpu.roll(win, sh, 0)
                sh *= 2
            win_sum = jnp.where(grp == gi, win[MAX_WINDOW:, :], win_sum)
            count = jnp.where(grp == gi, jnp.minimum(tpos + 1, w).astype(_F32), count)
        pooled = win_sum / count - p_pool
        y_pool = (_dot(pooled, poolw_ref[...]) + vec(_V_POOL_B)) * vec(_V_POOL_SCALE)
        y_ref[0, r0:r0 + sub, 0:gw] = y_pool.astype(y_ref.dtype)
        yield

        sm = jnp.exp(lb_ref[...] - jnp.max(lb_ref[...], axis=0, keepdims=True))
        sm = sm / jnp.sum(sm, axis=0, keepdims=True)
        lb = jnp.sum(sm[0:layer + 1], axis=0, keepdims=True) - sm[0:1]
        hq = pcol(_P_HG, 0)
        f = lb + (1.0 - lb) * _sigmoid(pcol(_P_HG, 1))
        put(_F_HQ, hq * _sigmoid(hq) * QK_SCALE)
        put(_F_HK, 1.0 - f)
        put(_F_HV, pcol(_P_HG, 2))
        put(_F_HGD, jnp.log(jnp.maximum(f, GATE_FLOOR)))
        yield

        p_rw = pcol(_P_RW, 0, 4)
        before = rw_prev[...] if r0 == 0 else p_ref[0, r0 - 1:r0, _P_RW:_P_RW + 4 * gw]
        row0 = lax.broadcasted_iota(jnp.int32, p_rw.shape, 0) == 0
        prev = jnp.where(row0, before, pltpu.roll(p_rw, 1, 0))
        if r0 + sub == tb:
            rw_prev[...] = p_rw[sub - 1:sub, :]
        p_rw = p_rw + (prev - p_rw) * mu_ref[...]
        rr = p_rw[:, 0:gw]
        rk = p_rw[:, gw:2 * gw]
        rv = p_rw[:, 2 * gw:3 * gw]
        xwa = p_rw[:, 3 * gw:3 * gw + LANES]
        xg = p_rw[:, 3 * gw + LANES:4 * gw]
        w_log = -_softplus(-(vec(_V_W0) + _dot(jnp.tanh(xwa), w2_ref[...]))) - 0.5
        yield
        a = _sigmoid(vec(_V_A0) + _dot(xwa, a2_ref[...]))
        yield
        aux[r0:r0 + sub, gw:2 * gw] = _dot(_sigmoid(xg), g2_ref[...])
        yield
        kk = rk * vec(_V_KK)
        kk = kk / jnp.maximum(jnp.sqrt(_dot(kk * kk, seg) * HEAD_DIM), 1e-12)
        yield
        rk = rk * (1.0 + (a - 1.0) * vec(_V_KA))
        put(_F_RR, rr)
        put(_F_RK, rk)
        put(_F_RV, rv)
        put(_F_RA, kk)
        put(_F_RB, kk * a)
        put(_F_RW, -jnp.exp(w_log))
        aux[r0:r0 + sub, 0:gw] = _dot(rr * rk * vec(_V_RK), seg) * HEAD_DIM * rv
        yield

        ga = p_ref[0, r0:r0 + sub, _P_GLA + 4 * gw:_P_GLA + 4 * gw + LANES]
        z = _dot(ga, glaw2_ref[...]) + vec(_V_GLA_B)
        put(_F_GQ, pcol(_P_GLA, 0) * QK_SCALE)
        put(_F_GK, pcol(_P_GLA, 1))
        put(_F_GV, pcol(_P_GLA, 2))
        put(_F_GGD, -_softplus(-z) / GLA_GATE_TAU)
        yield
        yield True

    def outputs(r0):
        def pcol(base, i):
            return p_ref[0, r0:r0 + sub, base + i * gw: base + (i + 1) * gw]

        o_h = osc[r0:r0 + sub, 0:gw]
        y_hg = _rms_norm(o_h, vec(_V_HGRN_NORM)) * _sigmoid(pcol(_P_HG, 3))
        y_ref[0, r0:r0 + sub, gw:2 * gw] = y_hg.astype(y_ref.dtype)
        yield
        o_r = osc[r0:r0 + sub, gw:2 * gw]
        cen = o_r - _dot(o_r, seg)
        yield
        var = _dot(cen * cen, seg)
        gn = cen * lax.rsqrt(var + RWKV_GN_EPS) * vec(_V_LNW) + vec(_V_LNB)
        y_rw = (gn + aux[r0:r0 + sub, 0:gw]) * aux[r0:r0 + sub, gw:2 * gw]
        y_ref[0, r0:r0 + sub, 2 * gw:3 * gw] = y_rw.astype(y_ref.dtype)
        yield
        o_g = osc[r0:r0 + sub, 2 * gw:3 * gw]
        ms = _dot(o_g * o_g, seg)
        gg = pcol(_P_GLA, 3)
        y_gl = o_g * lax.rsqrt(ms + NORM_EPS) * vec(_V_GLA_NORM) * (gg * _sigmoid(gg))
        y_ref[0, r0:r0 + sub, 3 * gw:4 * gw] = y_gl.astype(y_ref.dtype)
        yield
        yield True

    xr = lax.broadcasted_iota(jnp.int32, (c, c), 0) ^ lax.broadcasted_iota(jnp.int32, (c, c), 1)
    bd_c = (lax.shift_right_logical(lax.broadcasted_iota(jnp.int32, (hc, hc), 0), LEVELS)
            == lax.shift_right_logical(lax.broadcasted_iota(jnp.int32, (hc, hc), 1), LEVELS))
    row_g = lax.broadcasted_iota(jnp.int32, (gw, gw), 0)
    col_g = lax.broadcasted_iota(jnp.int32, (gw, gw), 1)
    bd_g = lax.shift_right_logical(row_g, HEAD_SHIFT) == lax.shift_right_logical(col_g, HEAD_SHIFT)
    eye_g = row_g == col_g
    sel = sel_ref[...]
    tri = tri_ref[...]
    dist_row = lax.broadcasted_iota(jnp.int32, (c, hc), 0) ^ (lax.broadcasted_iota(jnp.int32, (c, hc), 1) & (c - 1))
    dist_bd = ((lax.broadcasted_iota(jnp.int32, (hc, hc), 0) ^ lax.broadcasted_iota(jnp.int32, (hc, hc), 1))
               & (c - 1))
    inv_masks = {"base_row": dist_row < INVERSE_BASE, "base_bd": bd_c & (dist_bd < INVERSE_BASE), "merge_bd": []}
    m = INVERSE_BASE
    while m < c:
        inv_masks["merge_bd"].append(bd_c & (dist_bd >= m) & (dist_bd < 2 * m))
        m *= 2

    state = [rw_p[...], hg_st[...], gl_st[...]]
    feat_ready, pre_done, state_done = set(), {}, set()

    def features_task(s_i):
        for _ in features(s_i * sub):
            yield
        feat_ready.add(s_i)

    def chunk_task(s_i, u, kind):
        while s_i not in feat_ready:
            yield
        r0 = s_i * sub + u * c

        def ft(i):
            return feat[r0:r0 + c, i * gw:(i + 1) * gw]

        if kind == 0:
            gen = _rwkv_chunk(ft(_F_RR), ft(_F_RK), ft(_F_RV), ft(_F_RA), ft(_F_RB), ft(_F_RW),
                              tri, hms, bd_c, bd_g, eye_g, inv_masks)
        elif kind == 1:
            gen = _gla_chunk(ft(_F_HQ), ft(_F_HK), ft(_F_HV), ft(_F_HGD), sel, seg, hms, xr, bd_g)
        else:
            gen = _gla_chunk(ft(_F_GQ), ft(_F_GK), ft(_F_GV), ft(_F_GGD), sel, seg, hms, xr, bd_g)
        for out in gen:
            if out is None:
                yield
            else:
                pre_done[(s_i, u, kind)] = out

    def state_task(s_i):
        keys = [(s_i, u, kind) for u in range(CHUNK_UNROLL) for kind in range(3)]
        while (s_i > 0 and (s_i - 1) not in state_done) or any(k not in pre_done for k in keys):
            yield
        for u in range(CHUNK_UNROLL):
            r0 = s_i * sub + u * c
            o_r, state[0] = _rwkv_state_step(pre_done[(s_i, u, 0)], state[0])
            yield
            o_h, state[1] = _gla_state_step(pre_done[(s_i, u, 1)], state[1])
            yield
            o_g, state[2] = _gla_state_step(pre_done[(s_i, u, 2)], state[2])
            osc[r0:r0 + c, 0:gw] = o_h
            osc[r0:r0 + c, gw:2 * gw] = o_r
            osc[r0:r0 + c, 2 * gw:3 * gw] = o_g
            yield
        state_done.add(s_i)

    def outputs_task(s_i):
        while s_i not in state_done:
            yield
        for _ in outputs(s_i * sub):
            yield

    tasks = []
    for s_i in range(n_sub):
        tasks.append((max(0, s_i - 1) * GROUP_ROUNDS, features_task(s_i)))
        for u in range(CHUNK_UNROLL):
            for kind in range(3):
                tasks.append((s_i * GROUP_ROUNDS, chunk_task(s_i, u, kind)))
        tasks.append((s_i * GROUP_ROUNDS, state_task(s_i)))
        tasks.append((s_i * GROUP_ROUNDS, outputs_task(s_i)))
    _run_tasks(tasks)
    rw_p[...], hg_st[...], gl_st[...] = state


def _selection_matrix():
    c = CHUNK
    m = np.zeros((2 * c, c), np.float32)
    for t in range(c):
        mid = (t // 4) * 4 + 1
        if t % 4 >= 2:
            m[t, mid + 1:t + 1] = 1.0
        else:
            m[t, t + 1:mid + 1] = 1.0
        m[c + t, :t + 1] = 1.0
    return m


def _const_spec(shape):
    nd = len(shape)
    return pl.BlockSpec(shape, lambda *_: (0,) * nd, pipeline_mode=pl.Buffered(1))


def _layer_spec(shape, layer):
    return pl.BlockSpec((None,) + tuple(shape[1:]), lambda *_: (layer, 0, 0), pipeline_mode=pl.Buffered(1))


def _mixer_call(p, consts, layer):
    b, t, _ = p.shape
    tb = TIME_TILE
    gw = GROUP_W
    in_specs = [pl.BlockSpec((1, tb, D_IN_PAD), lambda bi, ti: (bi, ti, 0))]
    in_specs += [_const_spec(a.shape) for a in consts]
    return pl.pallas_call(
        functools.partial(_mixer_kernel, layer=layer),
        grid=(b, t // tb),
        in_specs=in_specs,
        out_specs=pl.BlockSpec((1, tb, D_MODEL), lambda bi, ti: (bi, ti, 0)),
        out_shape=jax.ShapeDtypeStruct((b, t, D_MODEL), _MXU_DTYPE),
        scratch_shapes=[
            pltpu.VMEM((tb, _N_FEAT * gw), _F32),
            pltpu.VMEM((tb, 3 * gw), _F32),
            pltpu.VMEM((tb, 2 * gw), _F32),
            pltpu.VMEM((MAX_WINDOW, gw), _F32),
            pltpu.VMEM((1, 4 * gw), _F32),
            pltpu.VMEM((gw, gw), _F32),
            pltpu.VMEM((gw, gw), _F32),
            pltpu.VMEM((gw, gw), _F32),
        ],
        compiler_params=pltpu.CompilerParams(
            dimension_semantics=("arbitrary", "arbitrary"), vmem_limit_bytes=VMEM_LIMIT_BYTES),
        name=f"mixer_l{layer}",
    )(p, *consts)


def _ffn_kernel(*refs, final, fused_proj):
    if fused_proj:
        x_ref, y_ref, wp_ref, g_ref, win_ref, wout_ref, gf_ref, o_ref = refs
        x = x_ref[...] + jnp.dot(y_ref[...], wp_ref[...], preferred_element_type=_F32)
    else:
        x_ref, g_ref, win_ref, wout_ref, gf_ref, o_ref = refs
        x = x_ref[...]
    xg, rs = _rms_norm_factors(x, g_ref[...])
    acc = jnp.zeros(x.shape, _F32)
    lo = 0
    for width in FF_BLOCKS:
        gate = jnp.dot(xg, win_ref[:, lo:lo + width], preferred_element_type=_F32) * rs
        up = jnp.dot(xg, win_ref[:, D_FF + lo:D_FF + lo + width], preferred_element_type=_F32) * rs
        act = _mx(gate * _sigmoid(gate) * up)
        acc = acc + jnp.dot(act, wout_ref[lo:lo + width, :], preferred_element_type=_F32)
        lo += width
    y = x + 0.5 * acc
    if final:
        y = _rms_norm(y, gf_ref[...])
    o_ref[...] = y


def _ffn_call(x2, g, w_in, w_out, g_final, final, layer, proj=None):
    n, d = x2.shape
    tm = ROW_TILE
    row_spec = pl.BlockSpec((tm, d), lambda i: (i, 0))
    args = [x2]
    in_specs = [row_spec]
    if proj is not None:
        y2, w_proj = proj
        args += [y2, w_proj]
        in_specs += [pl.BlockSpec((tm, y2.shape[1]), lambda i: (i, 0)), _layer_spec(w_proj.shape, layer)]
    args += [g, w_in, w_out, g_final]
    in_specs += [_const_spec(g.shape), _layer_spec(w_in.shape, layer), _layer_spec(w_out.shape, layer),
                 _const_spec(g_final.shape)]
    return pl.pallas_call(
        functools.partial(_ffn_kernel, final=final, fused_proj=proj is not None),
        grid=(n // tm,),
        in_specs=in_specs,
        out_specs=row_spec,
        out_shape=jax.ShapeDtypeStruct((n, d), _F32),
        compiler_params=pltpu.CompilerParams(
            dimension_semantics=("arbitrary",), vmem_limit_bytes=VMEM_LIMIT_BYTES),
        name="ffn_proj" if proj is not None else "ffn",
    )(*args)


def _inproj_kernel(x_ref, g_ref, w_ref, o_ref):
    xg, rs = _rms_norm_factors(x_ref[...], g_ref[...])
    o_ref[...] = jnp.dot(xg, w_ref[...], preferred_element_type=_F32) * rs


def _inproj_call(x2, g, w, layer):
    n, d = x2.shape
    tm = ROW_TILE
    return pl.pallas_call(
        _inproj_kernel,
        grid=(n // tm,),
        in_specs=[pl.BlockSpec((tm, d), lambda i: (i, 0)), _const_spec(g.shape), _layer_spec(w.shape, layer)],
        out_specs=pl.BlockSpec((tm, w.shape[2]), lambda i: (i, 0)),
        out_shape=jax.ShapeDtypeStruct((n, w.shape[2]), _F32),
        compiler_params=pltpu.CompilerParams(
            dimension_semantics=("arbitrary",), vmem_limit_bytes=VMEM_LIMIT_BYTES),
        name="inproj",
    )(x2, g, w)


def _pad_rows(w, rows):
    return jnp.pad(w, ((0, rows - w.shape[0]), (0, 0)))


def kernel(x, norm_ffn1, ffn1_w_in, ffn1_w_out, norm_mix, w_in, w_out, pool_w, pool_b, pool_scale,
           hgrn_lb_logits, hgrn_norm, rwkv_mu, rwkv_w0, rwkv_w2, rwkv_a0, rwkv_a2, rwkv_g2, rwkv_k_k,
           rwkv_k_a, rwkv_r_k, rwkv_ln_w, rwkv_ln_b, gla_w2, gla_b, gla_norm, norm_ffn2, ffn2_w_in,
           ffn2_w_out, norm_final):
    b, t, d = x.shape
    depth = norm_ffn1.shape[0]
    gw = GROUP_W
    assert d == D_MODEL and t % TIME_TILE == 0 and (b * t) % ROW_TILE == 0
    assert TIME_TILE % (CHUNK * CHUNK_UNROLL) == 0
    assert sum(FF_BLOCKS) == D_FF and all(w % MXU_TILE == 0 for w in FF_BLOCKS)

    sel = jnp.asarray(_selection_matrix(), _MXU_DTYPE)
    tri = jnp.asarray(np.tril(np.ones((CHUNK, CHUNK), np.float32)), _MXU_DTYPE)
    head = np.arange(gw) // HEAD_DIM
    seg = jnp.asarray((head[:, None] == head[None, :]).astype(np.float32) / HEAD_DIM, _MXU_DTYPE)
    bdmask = jnp.asarray((head[:, None] == head[None, :]).astype(np.float32))
    gf = norm_final.reshape(1, d)

    w1_in, w1_out, w2_in, w2_out, w_mix_out = (_mx(w) for w in (ffn1_w_in, ffn1_w_out, ffn2_w_in, ffn2_w_out, w_out))
    w_mix_in = _mx(jnp.pad(w_in, ((0, 0), (0, 0), (0, D_IN_PAD - D_IN))))

    h = x.reshape(b * t, d)
    for l in range(depth):
        h = _ffn_call(h, norm_ffn1[l].reshape(1, d), w1_in, w1_out, gf, False, l)

        p = _inproj_call(h, norm_mix[l].reshape(1, d), w_mix_in, l)
        poolw = _mx(jnp.tile(pool_w[l].reshape(gw, POOL_CH), (1, len(POOL_WINDOWS))) * bdmask)
        w2p = _mx(_pad_rows(rwkv_w2[l], LANES))
        a2p = _mx(jnp.pad(rwkv_a2[l], ((RWKV_DECAY_LORA, 0), (0, 0))))
        vecs = jnp.stack([pool_b[l], pool_scale[l], hgrn_norm[l], rwkv_w0[l], rwkv_a0[l], rwkv_k_k[l],
                          rwkv_k_a[l], rwkv_r_k[l], rwkv_ln_w[l], rwkv_ln_b[l], gla_b[l], gla_norm[l]])
        consts = (sel, tri, seg, poolw, w2p, a2p, _mx(rwkv_g2[l]), _mx(_pad_rows(gla_w2[l], LANES)),
                  rwkv_mu[l].reshape(1, 4 * gw), _pad_rows(vecs, _N_VEC), hgrn_lb_logits)
        y = _mixer_call(p.reshape(b, t, D_IN_PAD), consts, l)
        h = _ffn_call(h, norm_ffn2[l].reshape(1, d), w2_in, w2_out, gf, l == depth - 1, l,
                      proj=(y.reshape(b * t, d), w_mix_out))
    return h.reshape(b, t, d)
```
